```python
import jax, jax.numpy as jnp
from jax import lax
import numpy as np

D_MODEL = 1024
BATCH = 8
SEQ = 2048
DEPTH = 4
DEC_BATCH = 128
DEC_SEQ = 8
PAST_LEN = 16384
PAGE_SIZE = 128

N_POOL_GROUPS = 4
POOL_WINDOWS = (2, 4, 8, 16)
POOL_WIDTH = D_MODEL
POOL_GROUP_DIM = POOL_WIDTH // N_POOL_GROUPS
POOL_BUF = max(POOL_WINDOWS) - 1
CONV_WIDTH = D_MODEL
CONV_KERNEL = 31
CONV_BUF = CONV_KERNEL - 1
N_MEM = 256
XATTN_HEADS = 4
XATTN_HEAD_DIM = D_MODEL // XATTN_HEADS
XATTN_WIDTH = XATTN_HEADS * XATTN_HEAD_DIM
N_BRANCHES = 3
D_FF = 2816
EPS = 1e-6
IN_COLS = POOL_WIDTH + 2 * CONV_WIDTH + XATTN_WIDTH + N_BRANCHES * D_MODEL

kernel_name = "hybrid_pool_conv_xattn_macaron_decoder_step"


def rmsnorm(x, g):
    xf = x.astype(jnp.float32)
    y = xf * lax.rsqrt(jnp.mean(xf * xf, axis=-1, keepdims=True) + EPS)
    return (y * g.astype(jnp.float32)).astype(x.dtype)


def layernorm(x, g, b):
    xf = x.astype(jnp.float32)
    mu = jnp.mean(xf, axis=-1, keepdims=True)
    var = jnp.mean(jnp.square(xf - mu), axis=-1, keepdims=True)
    y = (xf - mu) * lax.rsqrt(var + EPS)
    return (y * g.astype(jnp.float32) + b.astype(jnp.float32)).astype(x.dtype)


def swiglu_ffn(h, w_gu, w_down):
    gate, up = jnp.split(h @ w_gu, 2, axis=-1)
    return (jax.nn.silu(gate) * up) @ w_down


def pool_mixer(u, buf, pos0, w_pool, pool_scale):
    B, T, _ = u.shape
    ext = jnp.concatenate([buf.astype(u.dtype), u], axis=1)
    cs = jnp.cumsum(ext.astype(jnp.float32), axis=1)
    cs = jnp.concatenate([jnp.zeros_like(cs[:, :1]), cs], axis=1)
    pos = pos0 + jnp.arange(T)
    means = []
    for gi, w in enumerate(POOL_WINDOWS):
        sl = slice(gi * POOL_GROUP_DIM, (gi + 1) * POOL_GROUP_DIM)
        hi = cs[:, POOL_BUF + 1:POOL_BUF + 1 + T, sl]
        lo = cs[:, POOL_BUF + 1 - w:POOL_BUF + 1 - w + T, sl]
        cnt = jnp.minimum(pos + 1, w).astype(jnp.float32)[None, :, None]
        means.append((hi - lo) / cnt)
    mean = jnp.stack(means, axis=2)
    pooled = (mean - u.reshape(B, T, N_POOL_GROUPS, POOL_GROUP_DIM).astype(jnp.float32)).astype(u.dtype)
    y = jnp.einsum('btgc,gcd->btgd', pooled, w_pool).reshape(B, T, POOL_WIDTH)
    return y * pool_scale, ext[:, -POOL_BUF:]


def conv_module(v, buf, w_dw, b_dw, ln_g, ln_b, w_pw):
    a, gate = jnp.split(v, 2, axis=-1)
    z = a * jax.nn.sigmoid(gate)
    ext = jnp.concatenate([buf.astype(z.dtype), z], axis=1)
    y = lax.conv_general_dilated(
        ext, w_dw[:, None, :].astype(z.dtype), window_strides=(1,), padding='VALID',
        dimension_numbers=('NWC', 'WIO', 'NWC'), feature_group_count=CONV_WIDTH)
    y = jax.nn.silu(layernorm(y + b_dw, ln_g, ln_b))
    return y @ w_pw, ext[:, -CONV_BUF:]


def memory_kv(mem, mem_g, w_kv):
    B = mem.shape[0]
    k, v = jnp.split(rmsnorm(mem, mem_g) @ w_kv, 2, axis=-1)
    shp = (B, N_MEM, XATTN_HEADS, XATTN_HEAD_DIM)
    return k.reshape(shp), v.reshape(shp)


def cross_attend(q, k, v):
    B, T = q.shape[:2]
    s = jnp.einsum('bthd,bmhd->bhtm', q, k.astype(q.dtype),
                   preferred_element_type=jnp.float32) * (XATTN_HEAD_DIM ** -0.5)
    p = jax.nn.softmax(s, axis=-1).astype(v.dtype)
    o = jnp.einsum('bhtm,bmhd->bthd', p, v)
    return o.reshape(B, T, XATTN_WIDTH).astype(q.dtype)


def trunk_layer(x, pos0, pool_buf, conv_buf, mem_k, mem_v,
                ffn1_norm, ffn1_w_gu, ffn1_w_down, mix_norm, w_in,
                w_pool, pool_scale, w_dw, b_dw, conv_ln_g, conv_ln_b, w_pw,
                w_out, ffn2_norm, ffn2_w_gu, ffn2_w_down):
    B, T, _ = x.shape
    x = x + 0.5 * swiglu_ffn(rmsnorm(x, ffn1_norm), ffn1_w_gu, ffn1_w_down)
    h = rmsnorm(x, mix_norm)
    u, v, q, gl = jnp.split(h @ w_in, [POOL_WIDTH, POOL_WIDTH + 2 * CONV_WIDTH,
                                        POOL_WIDTH + 2 * CONV_WIDTH + XATTN_WIDTH], axis=-1)
    y_pool, new_pool = pool_mixer(u, pool_buf, pos0, w_pool, pool_scale)
    y_conv, new_conv = conv_module(v, conv_buf, w_dw, b_dw, conv_ln_g, conv_ln_b, w_pw)
    y_att = cross_attend(q.reshape(B, T, XATTN_HEADS, XATTN_HEAD_DIM), mem_k, mem_v)
    gates = jax.nn.sigmoid(gl.reshape(B, T, N_BRANCHES, D_MODEL))
    merged = gates[:, :, 0] * y_pool + gates[:, :, 1] * y_conv + gates[:, :, 2] * y_att
    x = x + merged @ w_out
    x = x + 0.5 * swiglu_ffn(rmsnorm(x, ffn2_norm), ffn2_w_gu, ffn2_w_down)
    return x, new_pool, new_conv


def setup_inputs(seed: int = 0) -> dict:
    key = jax.random.key(seed)
    ks = jax.random.split(key, 32)
    f32 = jnp.float32

    def nrm(k, shape, scale):
        return jax.random.normal(k, shape, f32) * scale

    def gain(k, shape):
        return 1.0 + 0.05 * jax.random.normal(k, shape, f32)

    return {
        "x_prompt": nrm(ks[0], (BATCH, SEQ, D_MODEL), 1.0),
        "x_sample": nrm(ks[1], (DEC_BATCH, DEC_SEQ, D_MODEL), 1.0),
        "mem_prompt": nrm(ks[2], (BATCH, N_MEM, D_MODEL), 1.0),
        "state_pool": nrm(ks[3], (DEPTH, DEC_BATCH, POOL_BUF, POOL_WIDTH), 1.0),
        "state_conv": nrm(ks[4], (DEPTH, DEC_BATCH, CONV_BUF, CONV_WIDTH), 0.5),
        "cache_mem_k": nrm(ks[5], (DEPTH, DEC_BATCH, N_MEM, XATTN_HEADS, XATTN_HEAD_DIM), 1.0),
        "cache_mem_v": nrm(ks[6], (DEPTH, DEC_BATCH, N_MEM, XATTN_HEADS, XATTN_HEAD_DIM), 1.0),
        "ffn1_norm": gain(ks[7], (DEPTH, D_MODEL)),
        "ffn1_w_gu": nrm(ks[8], (DEPTH, D_MODEL, 2 * D_FF), D_MODEL ** -0.5),
        "ffn1_w_down": nrm(ks[9], (DEPTH, D_FF, D_MODEL), D_FF ** -0.5),
        "mix_norm": gain(ks[10], (DEPTH, D_MODEL)),
        "w_in": nrm(ks[11], (DEPTH, D_MODEL, IN_COLS), D_MODEL ** -0.5),
        "mem_norm": gain(ks[12], (DEPTH, D_MODEL)),
        "w_mem_kv": nrm(ks[13], (DEPTH, D_MODEL, 2 * XATTN_WIDTH), D_MODEL ** -0.5),
        "w_pool": nrm(ks[14], (DEPTH, N_POOL_GROUPS, POOL_GROUP_DIM, POOL_GROUP_DIM), POOL_GROUP_DIM ** -0.5),
        "pool_scale": 0.5 + 0.1 * jax.random.normal(ks[15], (DEPTH, POOL_WIDTH), f32),
        "w_dw": nrm(ks[16], (DEPTH, CONV_KERNEL, CONV_WIDTH), CONV_KERNEL ** -0.5),
        "b_dw": nrm(ks[17], (DEPTH, CONV_WIDTH), 0.02),
        "conv_ln_g": gain(ks[18], (DEPTH, CONV_WIDTH)),
        "conv_ln_b": nrm(ks[19], (DEPTH, CONV_WIDTH), 0.02),
        "w_pw": nrm(ks[20], (DEPTH, CONV_WIDTH, D_MODEL), CONV_WIDTH ** -0.5),
        "w_out": nrm(ks[21], (DEPTH, D_MODEL, D_MODEL), D_MODEL ** -0.5),
        "ffn2_norm": gain(ks[22], (DEPTH, D_MODEL)),
        "ffn2_w_gu": nrm(ks[23], (DEPTH, D_MODEL, 2 * D_FF), D_MODEL ** -0.5),
        "ffn2_w_down": nrm(ks[24], (DEPTH, D_FF, D_MODEL), D_FF ** -0.5),
        "final_norm": gain(ks[25], (D_MODEL,)),
    }


def reference(x_prompt, x_sample, mem_prompt, state_pool, state_conv, cache_mem_k, cache_mem_v,
              ffn1_norm, ffn1_w_gu, ffn1_w_down, mix_norm, w_in, mem_norm, w_mem_kv,
              w_pool, pool_scale, w_dw, b_dw, conv_ln_g, conv_ln_b, w_pw, w_out,
              ffn2_norm, ffn2_w_gu, ffn2_w_down, final_norm):
    xp, xs = x_prompt, x_sample
    pool_buf_p = jnp.zeros((BATCH, POOL_BUF, POOL_WIDTH), xp.dtype)
    conv_buf_p = jnp.zeros((BATCH, CONV_BUF, CONV_WIDTH), xp.dtype)
    pool_p, conv_p, memk_p, memv_p, pool_s, conv_s = [], [], [], [], [], []
    for l in range(DEPTH):
        shared = (ffn1_norm[l], ffn1_w_gu[l], ffn1_w_down[l], mix_norm[l], w_in[l],
                  w_pool[l], pool_scale[l], w_dw[l], b_dw[l], conv_ln_g[l], conv_ln_b[l], w_pw[l],
                  w_out[l], ffn2_norm[l], ffn2_w_gu[l], ffn2_w_down[l])
        mk, mv = memory_kv(mem_prompt, mem_norm[l], w_mem_kv[l])
        xp, npool, nconv = trunk_layer(xp, 0, pool_buf_p, conv_buf_p, mk, mv, *shared)
        pool_p.append(npool)
        conv_p.append(nconv)
        memk_p.append(mk)
        memv_p.append(mv)
        xs, spool, sconv = trunk_layer(xs, PAST_LEN, state_pool[l], state_conv[l],
                                       cache_mem_k[l], cache_mem_v[l], *shared)
        pool_s.append(spool)
        conv_s.append(sconv)
    y_prompt = rmsnorm(xp, final_norm)
    y_sample = rmsnorm(xs, final_norm)
    return (y_prompt, y_sample, jnp.stack(pool_p), jnp.stack(conv_p), jnp.stack(memk_p),
            jnp.stack(memv_p), jnp.stack(pool_s), jnp.stack(conv_s))
```

```python
import functools

import jax
import jax.numpy as jnp
from jax import lax
from jax.experimental import pallas as pl
from jax.experimental.pallas import tpu as pltpu

F32 = jnp.float32
BF = jnp.bfloat16

D = 1024
DFF = 2816
DEPTH = 4
BATCH, SEQ = 8, 2048
DEC_BATCH, DEC_SEQ = 128, 8
PAST_LEN = 16384
WINDOWS = (2, 4, 8, 16)
GD = D // len(WINDOWS)
PB = max(WINDOWS) - 1
CK = 31
CB = CK - 1
NMEM = 256
NH = 4
DH = D // NH
EPS = 1e-6
NP = BATCH * SEQ
NS = DEC_BATCH * DEC_SEQ
NT = NP + NS

SUBLANES = 8
LANES = 128
POOL_HALO = 16
CONV_HALO = 32
VMEM_LIMIT = 58 * 1024 * 1024

TM_FFN = 512
FFN_CHUNKS = (512, 512, 512, 512, 512, 256)
TM_MIX = 256
SB_PRE = 16
SB_ATT = 8
TM_KV = 512


def _dot(a, b):
    return jnp.dot(a, b, preferred_element_type=F32)


def _rms(x, g):
    return x * lax.rsqrt(jnp.mean(x * x, axis=-1, keepdims=True) + EPS) * g


def _silu(x):
    return x * jax.nn.sigmoid(x)


def _layernorm(y, g, b):
    mu = jnp.mean(y, axis=-1, keepdims=True)
    yc = y - mu
    var = jnp.mean(yc * yc, axis=-1, keepdims=True)
    return yc * lax.rsqrt(var + EPS) * g + b


def _softmax(s):
    e = jnp.exp(s - jnp.max(s, axis=-1, keepdims=True))
    return e / jnp.sum(e, axis=-1, keepdims=True)


def _const_spec(shape, layer=None):
    if layer is None:
        return pl.BlockSpec(shape, lambda *_: (0,) * len(shape),
                            pipeline_mode=pl.Buffered(1))
    return pl.BlockSpec((None,) + shape, lambda *_: (layer,) + (0,) * len(shape),
                        pipeline_mode=pl.Buffered(1))


def _memkv_kernel(m_ref, g_ref, w_ref, k_ref, v_ref, kb_ref, vb_ref):
    h = _rms(m_ref[...], g_ref[...]).astype(BF)
    k = _dot(h, w_ref[:, 0:D])
    v = _dot(h, w_ref[:, D:2 * D])
    k_ref[...] = k
    v_ref[...] = v
    kb_ref[...] = k.astype(BF)
    vb_ref[...] = v.astype(BF)


def _memory_kv(mem, mem_norm, w_kv):
    rows = mem.shape[0]
    out = jax.ShapeDtypeStruct((DEPTH, rows, D), F32)
    outb = jax.ShapeDtypeStruct((DEPTH, rows, D), BF)
    row_spec = pl.BlockSpec((None, TM_KV, D), lambda l, i: (l, i, 0))
    return pl.pallas_call(
        _memkv_kernel,
        grid=(DEPTH, rows // TM_KV),
        in_specs=[pl.BlockSpec((TM_KV, D), lambda l, i: (i, 0)),
                  pl.BlockSpec((None, 1, D), lambda l, i: (l, 0, 0)),
                  pl.BlockSpec((None, D, 2 * D), lambda l, i: (l, 0, 0))],
        out_specs=[row_spec, row_spec, row_spec, row_spec],
        out_shape=[out, out, outb, outb],
        compiler_params=pltpu.CompilerParams(
            dimension_semantics=("arbitrary", "arbitrary"), vmem_limit_bytes=VMEM_LIMIT),
        name="memory_kv",
    )(mem, mem_norm, w_kv)


def _ffn_kernel(*refs, final):
    if final:
        x_ref, g_ref, wgu_ref, wd_ref, fg_ref, o_ref, a_ref = refs
    else:
        x_ref, g_ref, wgu_ref, wd_ref, o_ref, a_ref = refs
    x = x_ref[...]
    h = _rms(x, g_ref[...]).astype(BF)
    c0 = 0
    for width in FFN_CHUNKS:
        gate = _dot(h, wgu_ref[:, c0:c0 + width])
        up = _dot(h, wgu_ref[:, DFF + c0:DFF + c0 + width])
        a_ref[:, c0:c0 + width] = (_silu(gate) * up).astype(BF)
        c0 += width
    y = x + 0.5 * _dot(a_ref[...], wd_ref[...])
    if final:
        y = _rms(y, fg_ref[...])
    o_ref[...] = y


def _ffn(x, norm, w_gu, w_down, layer, final_norm=None):
    final = final_norm is not None
    tok_spec = pl.BlockSpec((TM_FFN, D), lambda i: (i, 0))
    in_specs = [tok_spec, _const_spec((1, D), layer), _const_spec((D, 2 * DFF), layer),
                _const_spec((DFF, D), layer)]
    args = [x, norm, w_gu, w_down]
    if final:
        in_specs.append(_const_spec((1, D)))
        args.append(final_norm)
    return pl.pallas_call(
        functools.partial(_ffn_kernel, final=final),
        grid=(NT // TM_FFN,),
        in_specs=in_specs,
        out_specs=tok_spec,
        out_shape=jax.ShapeDtypeStruct((NT, D), F32),
        scratch_shapes=[pltpu.VMEM((TM_FFN, DFF), BF)],
        compiler_params=pltpu.CompilerParams(
            dimension_semantics=("arbitrary",), vmem_limit_bytes=VMEM_LIMIT),
        name="ffn_final" if final else "ffn",
    )(*args)


def _conv_taps_by_shift():
    return [[k for k in range(b, CK, SUBLANES)] for b in range(SUBLANES)]


def _conv_ln_pw(y, bdw_ref, lng_ref, lnb_ref, wpw_ref):
    yn = _layernorm(y + bdw_ref[...], lng_ref[...], lnb_ref[...])
    return _dot(_silu(yn).astype(BF), wpw_ref[...])


CONV_ROWS = 64


def _prompt_mix_kernel(x_ref, g_ref, win_ref, wpool_ref, ps_ref, wdw_ref, bdw_ref, lng_ref,
                       lnb_ref, wpw_ref, wout_ref, k_ref, v_ref,
                       xo_ref, pool_o_ref, conv_o_ref, extu_ref, extz_ref, y_ref):
    tm = TM_MIX
    c = pl.program_id(1)

    @pl.when(c == 0)
    def _():
        extu_ref[0:POOL_HALO, :] = jnp.zeros((POOL_HALO, D), F32)
        extz_ref[0:CONV_HALO, :] = jnp.zeros((CONV_HALO, D), F32)

    x = x_ref[...]
    h = _rms(x, g_ref[...]).astype(BF)

    u = _dot(h, win_ref[:, 0:D])
    extu_ref[POOL_HALO:POOL_HALO + tm, :] = u
    pos = c * tm + lax.broadcasted_iota(jnp.int32, (tm, 1), 0)
    ys = []
    for gi, w in enumerate(WINDOWS):
        cols = slice(gi * GD, (gi + 1) * GD)
        s = u[:, cols]
        for j in range(1, w):
            s = s + extu_ref[POOL_HALO - j:POOL_HALO - j + tm, cols]
        inv_cnt = 1.0 / jnp.minimum(pos + 1, w).astype(F32)
        pooled = (s * inv_cnt - u[:, cols]).astype(BF)
        ys.append(_dot(pooled, wpool_ref[gi]))
    y_pool = jnp.concatenate(ys, axis=-1) * ps_ref[...]
    merged = jax.nn.sigmoid(_dot(h, win_ref[:, 4 * D:5 * D])) * y_pool
    tail_u = extu_ref[tm + POOL_HALO - PB:tm + POOL_HALO, :]
    pool_o_ref[...] = tail_u
    extu_ref[POOL_HALO - PB:POOL_HALO, :] = tail_u

    z = _dot(h, win_ref[:, D:2 * D]) * jax.nn.sigmoid(_dot(h, win_ref[:, 2 * D:3 * D]))
    extz_ref[CONV_HALO:CONV_HALO + tm, :] = z
    first = CONV_HALO - CB
    for cb in range(D // LANES):
        cols = slice(cb * LANES, (cb + 1) * LANES)
        for r0 in range(0, tm, CONV_ROWS):
            acc = None
            for b, taps in enumerate(_conv_taps_by_shift()):
                span = CONV_ROWS + SUBLANES * (len(taps) - 1)
                eb = extz_ref[first + r0 + b:first + r0 + b + span, cols]
                for a, k in enumerate(taps):
                    term = eb[SUBLANES * a:SUBLANES * a + CONV_ROWS, :] * wdw_ref[k:k + 1, cols]
                    acc = term if acc is None else acc + term
            y_ref[r0:r0 + CONV_ROWS, cols] = acc
    y_conv = _conv_ln_pw(y_ref[...], bdw_ref, lng_ref, lnb_ref, wpw_ref)
    merged = merged + jax.nn.sigmoid(_dot(h, win_ref[:, 5 * D:6 * D])) * y_conv
    tail_z = extz_ref[tm + CONV_HALO - CB:tm + CONV_HALO, :]
    conv_o_ref[...] = tail_z
    extz_ref[CONV_HALO - CB:CONV_HALO, :] = tail_z

    q = _dot(h, win_ref[:, 3 * D:4 * D])
    outs = []
    for hd in range(NH):
        cols = slice(hd * DH, (hd + 1) * DH)
        s = lax.dot_general(q[:, cols].astype(BF), k_ref[:, cols], (((1,), (1,)), ((), ())),
                            preferred_element_type=F32) * (DH ** -0.5)
        outs.append(_dot(_softmax(s).astype(BF), v_ref[:, cols]))
    y_att = jnp.concatenate(outs, axis=-1)
    merged = merged + jax.nn.sigmoid(_dot(h, win_ref[:, 6 * D:7 * D])) * y_att

    xo_ref[...] = x + _dot(merged.astype(BF), wout_ref[...])


def _prompt_mix(x, layer, mix_norm, w_in, w_pool, pool_scale, w_dw, b_dw, ln_g, ln_b, w_pw,
                w_out, memk_b, memv_b):
    nc = SEQ // TM_MIX
    tok_spec = pl.BlockSpec((TM_MIX, D), lambda b, c: (b * nc + c, 0))
    mem_spec = pl.BlockSpec((None, NMEM, D), lambda b, c: (layer, b, 0))
    vec = _const_spec((1, D), layer)
    return pl.pallas_call(
        _prompt_mix_kernel,
        grid=(BATCH, nc),
        in_specs=[tok_spec, vec, _const_spec((D, 7 * D), layer),
                  _const_spec((len(WINDOWS), GD, GD), layer), vec, _const_spec((CK, D), layer),
                  vec, vec, vec, _const_spec((D, D), layer), _const_spec((D, D), layer),
                  mem_spec, mem_spec],
        out_specs=[tok_spec,
                   pl.BlockSpec((None, PB, D), lambda b, c: (b, 0, 0)),
                   pl.BlockSpec((None, CB, D), lambda b, c: (b, 0, 0))],
        out_shape=[jax.ShapeDtypeStruct((NT, D), F32),
                   jax.ShapeDtypeStruct((BATCH, PB, D), F32),
                   jax.ShapeDtypeStruct((BATCH, CB, D), F32)],
        scratch_shapes=[pltpu.VMEM((POOL_HALO + TM_MIX, D), F32),
                        pltpu.VMEM((CONV_HALO + TM_MIX, D), F32),
                        pltpu.VMEM((TM_MIX, D), F32)],
        input_output_aliases={0: 0},
        compiler_params=pltpu.CompilerParams(
            dimension_semantics=("arbitrary", "arbitrary"), vmem_limit_bytes=VMEM_LIMIT),
        name="prompt_mix",
    )(x, mix_norm, w_in, w_pool, pool_scale, w_dw, b_dw, ln_g, ln_b, w_pw, w_out,
      memk_b, memv_b)


CONV_SEQS = 8


def _sample_pre_kernel(x_ref, g_ref, win_ref, wpool_ref, ps_ref, wdw_ref, bdw_ref, lng_ref,
                       lnb_ref, wpw_ref, sp_ref, sc_ref,
                       part_ref, q_ref, gc_ref, pool_o_ref, conv_o_ref,
                       extu_ref, extz_ref, y_ref):
    sb, t = SB_PRE, DEC_SEQ
    x = x_ref[...]
    h = _rms(x, g_ref[...]).astype(BF)

    u = _dot(h, win_ref[:, 0:D])
    u3 = u.reshape(sb, t, D)
    extu_ref[:, POOL_HALO - PB:POOL_HALO, :] = sp_ref[...]
    extu_ref[:, POOL_HALO:POOL_HALO + t, :] = u3
    pos = PAST_LEN + lax.broadcasted_iota(jnp.int32, (1, t, 1), 1)
    ys = []
    for gi, w in enumerate(WINDOWS):
        cols = slice(gi * GD, (gi + 1) * GD)
        s = u3[:, :, cols]
        for j in range(1, w):
            s = s + extu_ref[:, POOL_HALO - j:POOL_HALO - j + t, cols]
        inv_cnt = 1.0 / jnp.minimum(pos + 1, w).astype(F32)
        pooled = (s * inv_cnt - u3[:, :, cols]).reshape(sb * t, GD).astype(BF)
        ys.append(_dot(pooled, wpool_ref[gi]))
    y_pool = jnp.concatenate(ys, axis=-1) * ps_ref[...]
    part = jax.nn.sigmoid(_dot(h, win_ref[:, 4 * D:5 * D])) * y_pool
    pool_o_ref[...] = extu_ref[:, POOL_HALO + t - PB:POOL_HALO + t, :]

    z = _dot(h, win_ref[:, D:2 * D]) * jax.nn.sigmoid(_dot(h, win_ref[:, 2 * D:3 * D]))
    extz_ref[:, CONV_HALO - CB:CONV_HALO, :] = sc_ref[...]
    extz_ref[:, CONV_HALO:CONV_HALO + t, :] = z.reshape(sb, t, D)
    first = CONV_HALO - CB
    for cb in range(D // LANES):
        cols = slice(cb * LANES, (cb + 1) * LANES)
        for s0 in range(0, sb, CONV_SEQS):
            acc = None
            for b, taps in enumerate(_conv_taps_by_shift()):
                span = t + SUBLANES * (len(taps) - 1)
                eb = extz_ref[s0:s0 + CONV_SEQS, first + b:first + b + span, cols]
                for a, k in enumerate(taps):
                    term = eb[:, SUBLANES * a:SUBLANES * a + t, :] * wdw_ref[k:k + 1, cols]
                    acc = term if acc is None else acc + term
            y_ref[s0:s0 + CONV_SEQS, :, cols] = acc
    y_conv = _conv_ln_pw(y_ref[...].reshape(sb * t, D), bdw_ref, lng_ref, lnb_ref, wpw_ref)
    part_ref[...] = part + jax.nn.sigmoid(_dot(h, win_ref[:, 5 * D:6 * D])) * y_conv
    conv_o_ref[...] = extz_ref[:, CONV_HALO + t - CB:CONV_HALO + t, :]

    q_ref[...] = _dot(h, win_ref[:, 3 * D:4 * D])
    gc_ref[...] = jax.nn.sigmoid(_dot(h, win_ref[:, 6 * D:7 * D]))


def _sample_pre(x, layer, mix_norm, w_in, w_pool, pool_scale, w_dw, b_dw, ln_g, ln_b, w_pw,
                state_pool, state_conv):
    rows = SB_PRE * DEC_SEQ
    first_blk = NP // rows
    vec = _const_spec((1, D), layer)
    tok_out = pl.BlockSpec((rows, D), lambda i: (i, 0))
    tok_sds = jax.ShapeDtypeStruct((NS, D), F32)
    return pl.pallas_call(
        _sample_pre_kernel,
        grid=(DEC_BATCH // SB_PRE,),
        in_specs=[pl.BlockSpec((rows, D), lambda i: (first_blk + i, 0)), vec,
                  _const_spec((D, 7 * D), layer), _const_spec((len(WINDOWS), GD, GD), layer),
                  vec, _const_spec((CK, D), layer), vec, vec, vec, _const_spec((D, D), layer),
                  pl.BlockSpec((None, SB_PRE, PB, D), lambda i: (layer, i, 0, 0)),
                  pl.BlockSpec((None, SB_PRE, CB, D), lambda i: (layer, i, 0, 0))],
        out_specs=[tok_out, tok_out, tok_out,
                   pl.BlockSpec((SB_PRE, PB, D), lambda i: (i, 0, 0)),
                   pl.BlockSpec((SB_PRE, CB, D), lambda i: (i, 0, 0))],
        out_shape=[tok_sds, tok_sds, tok_sds,
                   jax.ShapeDtypeStruct((DEC_BATCH, PB, D), F32),
                   jax.ShapeDtypeStruct((DEC_BATCH, CB, D), F32)],
        scratch_shapes=[pltpu.VMEM((SB_PRE, POOL_HALO + DEC_SEQ, D), F32),
                        pltpu.VMEM((SB_PRE, CONV_HALO + DEC_SEQ, D), F32),
                        pltpu.VMEM((SB_PRE, DEC_SEQ, D), F32)],
        compiler_params=pltpu.CompilerParams(
            dimension_semantics=("arbitrary",), vmem_limit_bytes=VMEM_LIMIT),
        name="sample_pre",
    )(x, mix_norm, w_in, w_pool, pool_scale, w_dw, b_dw, ln_g, ln_b, w_pw,
      state_pool, state_conv)


def _sample_att_kernel(x_ref, part_ref, q_ref, gc_ref, k_ref, v_ref, wout_ref, xo_ref):
    sb, t = SB_ATT, DEC_SEQ
    q3 = q_ref[...].reshape(sb, t, D)
    outs = []
    for hd in range(NH):
        cols = slice(hd * DH, (hd + 1) * DH)
        s = jnp.einsum("std,smd->stm", q3[:, :, cols].astype(BF), k_ref[:, :, cols].astype(BF),
                       preferred_element_type=F32) * (DH ** -0.5)
        outs.append(jnp.einsum("stm,smd->std", _softmax(s).astype(BF),
                               v_ref[:, :, cols].astype(BF), preferred_element_type=F32))
    y_att = jnp.concatenate(outs, axis=-1).reshape(sb * t, D)
    merged = part_ref[...] + gc_ref[...] * y_att
    xo_ref[...] = x_ref[...] + _dot(merged.astype(BF), wout_ref[...])


def _sample_att(x, layer, part, q, gc, cache_k, cache_v, w_out):
    rows = SB_ATT * DEC_SEQ
    first_blk = NP // rows
    x_spec = pl.BlockSpec((rows, D), lambda i: (first_blk + i, 0))
    tok = pl.BlockSpec((rows, D), lambda i: (i, 0))
    kv = pl.BlockSpec((None, SB_ATT, NMEM, D), lambda i: (layer, i, 0, 0))
    return pl.pallas_call(
        _sample_att_kernel,
        grid=(DEC_BATCH // SB_ATT,),
        in_specs=[x_spec, tok, tok, tok, kv, kv, _const_spec((D, D), layer)],
        out_specs=x_spec,
        out_shape=jax.ShapeDtypeStruct((NT, D), F32),
        input_output_aliases={0: 0},
        compiler_params=pltpu.CompilerParams(
            dimension_semantics=("arbitrary",), vmem_limit_bytes=VMEM_LIMIT),
        name="sample_att",
    )(x, part, q, gc, cache_k, cache_v, w_out)


def kernel(x_prompt, x_sample, mem_prompt, state_pool, state_conv, cache_mem_k, cache_mem_v,
           ffn1_norm, ffn1_w_gu, ffn1_w_down, mix_norm, w_in, mem_norm, w_mem_kv, w_pool,
           pool_scale, w_dw, b_dw, conv_ln_g, conv_ln_b, w_pw, w_out, ffn2_norm, ffn2_w_gu,
           ffn2_w_down, final_norm):
    assert x_prompt.shape == (BATCH, SEQ, D) and x_sample.shape == (DEC_BATCH, DEC_SEQ, D)
    x = jnp.concatenate([x_prompt.reshape(NP, D), x_sample.reshape(NS, D)], axis=0)

    def vec(a):
        return a.reshape(DEPTH, 1, D)

    ffn1_norm, mix_norm, mem_norm, pool_scale, b_dw, conv_ln_g, conv_ln_b, ffn2_norm = map(
        vec, (ffn1_norm, mix_norm, mem_norm, pool_scale, b_dw, conv_ln_g, conv_ln_b, ffn2_norm))
    ffn1_w_gu, ffn1_w_down, w_in, w_mem_kv, w_pool, w_pw, w_out, ffn2_w_gu, ffn2_w_down = (
        a.astype(BF) for a in (ffn1_w_gu, ffn1_w_down, w_in, w_mem_kv, w_pool, w_pw, w_out,
                               ffn2_w_gu, ffn2_w_down))
    cache_k = cache_mem_k.reshape(DEPTH, DEC_BATCH, NMEM, D)
    cache_v = cache_mem_v.reshape(DEPTH, DEC_BATCH, NMEM, D)

    memk, memv, memk_b, memv_b = _memory_kv(mem_prompt.reshape(BATCH * NMEM, D), mem_norm,
                                            w_mem_kv)

    pool_p, conv_p, pool_s, conv_s = [], [], [], []
    for l in range(DEPTH):
        x = _ffn(x, ffn1_norm, ffn1_w_gu, ffn1_w_down, l)
        mix_w = (mix_norm, w_in, w_pool, pool_scale, w_dw, b_dw, conv_ln_g, conv_ln_b, w_pw)
        part, q, gc, spool, sconv = _sample_pre(x, l, *mix_w, state_pool, state_conv)
        x, ppool, pconv = _prompt_mix(x, l, *mix_w, w_out, memk_b, memv_b)
        x = _sample_att(x, l, part, q, gc, cache_k, cache_v, w_out)
        pool_p.append(ppool)
        conv_p.append(pconv)
        pool_s.append(spool)
        conv_s.append(sconv)
        x = _ffn(x, ffn2_norm, ffn2_w_gu, ffn2_w_down, l,
                 final_norm=final_norm.reshape(1, D) if l == DEPTH - 1 else None)

    kv_shape = (DEPTH, BATCH, NMEM, NH, DH)
    return (x[:NP].reshape(BATCH, SEQ, D), x[NP:].reshape(DEC_BATCH, DEC_SEQ, D),
            jnp.stack(pool_p), jnp.stack(conv_p), memk.reshape(kv_shape), memv.reshape(kv_shape),
            jnp.stack(pool_s), jnp.stack(conv_s))
```

```python
import functools

import jax
import jax.numpy as jnp
from jax import lax
from jax.experimental import pallas as pl
from jax.experimental.pallas import tpu as pltpu

F32 = jnp.float32
BF = jnp.bfloat16

D = 1024
DFF = 2816
DEPTH = 4
BATCH, SEQ = 8, 2048
DEC_BATCH, DEC_SEQ = 128, 8
PAST_LEN = 16384
WINDOWS = (2, 4, 8, 16)
GD = D // len(WINDOWS)
PB = max(WINDOWS) - 1
CK = 31
CB = CK - 1
NMEM = 256
NH = 4
DH = D // NH
EPS = 1e-6
NP = BATCH * SEQ
NS = DEC_BATCH * DEC_SEQ
NT = NP + NS

SUBLANES = 8
LANES = 128
NSLAB = D // LANES
POOL_HALO = 16
CONV_HALO = 32
VMEM_LIMIT = 58 * 1024 * 1024

TM_FFN = 512
FFN_CHUNKS = (512, 512, 512, 512, 512, 256)
TM_MIX = 256
CONV_ROWS = 64
SB_PRE = 16
SB_ATT = 8
TM_KV = 512
KV_ROWS = NMEM * NH * DH // LANES


def _dot(a, b):
    return jnp.dot(a, b, preferred_element_type=F32)


def _dot_t(a, b):
    return lax.dot_general(a, b, (((1,), (1,)), ((), ())), preferred_element_type=F32)


def _rms(x, g):
    return x * lax.rsqrt(jnp.mean(x * x, axis=-1, keepdims=True) + EPS) * g


def _silu(x):
    return x * jax.nn.sigmoid(x)


def _layernorm(y, g, b):
    mu = jnp.mean(y, axis=-1, keepdims=True)
    yc = y - mu
    var = jnp.mean(yc * yc, axis=-1, keepdims=True)
    return yc * lax.rsqrt(var + EPS) * g + b


def _softmax(s):
    e = jnp.exp(s - jnp.max(s, axis=-1, keepdims=True))
    return e / jnp.sum(e, axis=-1, keepdims=True)


def _conv_ln_pw(y, bdw_ref, lng_ref, lnb_ref, wpw_ref):
    yn = _layernorm(y + bdw_ref[...], lng_ref[...], lnb_ref[...])
    return _dot(_silu(yn).astype(BF), wpw_ref[...])


def _slab(cb):
    return slice(cb * LANES, (cb + 1) * LANES)


def _slab_rows(ref, cb, start, n):
    return ref[pl.ds(cb, 1, stride=2), pl.ds(start, n), :].reshape(n, LANES)


def _const_spec(shape, layer=None):
    if layer is None:
        return pl.BlockSpec(shape, lambda *_: (0,) * len(shape),
                            pipeline_mode=pl.Buffered(1))
    return pl.BlockSpec((None,) + shape, lambda *_: (layer,) + (0,) * len(shape),
                        pipeline_mode=pl.Buffered(1))


def _memkv_kernel(m_ref, g_ref, w_ref, k_ref, v_ref, kb_ref, vb_ref):
    h = _rms(m_ref[...], g_ref[...]).astype(BF)
    k = _dot(h, w_ref[:, 0:D])
    v = _dot(h, w_ref[:, D:2 * D])
    k_ref[...] = k
    v_ref[...] = v
    kb_ref[...] = k.astype(BF)
    vb_ref[...] = v.astype(BF)


def _memory_kv(mem, mem_norm, w_kv):
    rows = mem.shape[0]
    out = jax.ShapeDtypeStruct((DEPTH, rows, D), F32)
    outb = jax.ShapeDtypeStruct((DEPTH, rows, D), BF)
    row_spec = pl.BlockSpec((None, TM_KV, D), lambda l, i: (l, i, 0))
    return pl.pallas_call(
        _memkv_kernel,
        grid=(DEPTH, rows // TM_KV),
        in_specs=[pl.BlockSpec((TM_KV, D), lambda l, i: (i, 0)),
                  pl.BlockSpec((None, 1, D), lambda l, i: (l, 0, 0)),
                  pl.BlockSpec((None, D, 2 * D), lambda l, i: (l, 0, 0))],
        out_specs=[row_spec, row_spec, row_spec, row_spec],
        out_shape=[out, out, outb, outb],
        compiler_params=pltpu.CompilerParams(
            dimension_semantics=("arbitrary", "arbitrary"), vmem_limit_bytes=VMEM_LIMIT),
        name="memory_kv",
    )(mem, mem_norm, w_kv)


def _ffn_kernel(*refs, final):
    if final:
        x_ref, g_ref, wgu_ref, wd_ref, fg_ref, o_ref, a_ref = refs
    else:
        x_ref, g_ref, wgu_ref, wd_ref, o_ref, a_ref = refs
    x = x_ref[...]
    h = _rms(x, g_ref[...]).astype(BF)
    c0 = 0
    for width in FFN_CHUNKS:
        gate = _dot(h, wgu_ref[:, c0:c0 + width])
        up = _dot(h, wgu_ref[:, DFF + c0:DFF + c0 + width])
        a_ref[:, c0:c0 + width] = (_silu(gate) * up).astype(BF)
        c0 += width
    y = x + 0.5 * _dot(a_ref[...], wd_ref[...])
    if final:
        y = _rms(y, fg_ref[...])
    o_ref[...] = y


def _ffn(x, norm, w_gu, w_down, layer, final_norm=None):
    final = final_norm is not None
    tok_spec = pl.BlockSpec((TM_FFN, D), lambda i: (i, 0))
    in_specs = [tok_spec, _const_spec((1, D), layer), _const_spec((D, 2 * DFF), layer),
                _const_spec((DFF, D), layer)]
    args = [x, norm, w_gu, w_down]
    if final:
        in_specs.append(_const_spec((1, D)))
        args.append(final_norm)
    return pl.pallas_call(
        functools.partial(_ffn_kernel, final=final),
        grid=(NT // TM_FFN,),
        in_specs=in_specs,
        out_specs=tok_spec,
        out_shape=jax.ShapeDtypeStruct((NT, D), F32),
        scratch_shapes=[pltpu.VMEM((TM_FFN, DFF), BF)],
        compiler_params=pltpu.CompilerParams(
            dimension_semantics=("arbitrary",), vmem_limit_bytes=VMEM_LIMIT),
        name="ffn_final" if final else "ffn",
    )(*args)


def _prompt_mix_kernel(x_ref, g_ref, win_ref, wpool_ref, ps_ref, wdw_ref, bdw_ref, lng_ref,
                       lnb_ref, wpw_ref, wout_ref, k_ref, v_ref,
                       xo_ref, pool_o_ref, conv_o_ref, extu_ref, extz_ref, y_ref):
    tm = TM_MIX
    c = pl.program_id(1)

    @pl.when(c == 0)
    def _():
        extu_ref[:, 0:POOL_HALO, :] = jnp.zeros((NSLAB, POOL_HALO, LANES), F32)
        extz_ref[:, 0:CONV_HALO, :] = jnp.zeros((NSLAB, CONV_HALO, LANES), F32)

    x = x_ref[...]
    h = _rms(x, g_ref[...]).astype(BF)

    u = _dot(h, win_ref[:, 0:D])
    for cb in range(NSLAB):
        extu_ref[cb, POOL_HALO:POOL_HALO + tm, :] = u[:, _slab(cb)]
    pos = c * tm + lax.broadcasted_iota(jnp.int32, (tm, 1), 0)
    ys = []
    for gi, w in enumerate(WINDOWS):
        inv_cnt = 1.0 / jnp.minimum(pos + 1, w).astype(F32)
        parts = []
        for cb in range(gi * GD // LANES, (gi + 1) * GD // LANES):
            s = u[:, _slab(cb)]
            for j in range(1, w):
                s = s + _slab_rows(extu_ref, cb, POOL_HALO - j, tm)
            parts.append(s * inv_cnt - u[:, _slab(cb)])
        pooled = jnp.concatenate(parts, axis=-1).astype(BF)
        ys.append(_dot(pooled, wpool_ref[gi]))
    y_pool = jnp.concatenate(ys, axis=-1) * ps_ref[...]
    merged = jax.nn.sigmoid(_dot(h, win_ref[:, 4 * D:5 * D])) * y_pool
    for cb in range(NSLAB):
        tail_u = _slab_rows(extu_ref, cb, tm + POOL_HALO - PB, PB)
        pool_o_ref[:, _slab(cb)] = tail_u
        extu_ref[cb, POOL_HALO - PB:POOL_HALO, :] = tail_u

    z = _dot(h, win_ref[:, D:2 * D]) * jax.nn.sigmoid(_dot(h, win_ref[:, 2 * D:3 * D]))
    for cb in range(NSLAB):
        extz_ref[cb, CONV_HALO:CONV_HALO + tm, :] = z[:, _slab(cb)]
    first = CONV_HALO - CB
    for cb in range(NSLAB):
        for r0 in range(0, tm, CONV_ROWS):
            acc = None
            for k in range(CK):
                term = _slab_rows(extz_ref, cb, first + r0 + k, CONV_ROWS) * wdw_ref[k:k + 1,
                                                                                    _slab(cb)]
                acc = term if acc is None else acc + term
            y_ref[r0:r0 + CONV_ROWS, _slab(cb)] = acc
    y_conv = _conv_ln_pw(y_ref[...], bdw_ref, lng_ref, lnb_ref, wpw_ref)
    merged = merged + jax.nn.sigmoid(_dot(h, win_ref[:, 5 * D:6 * D])) * y_conv
    for cb in range(NSLAB):
        tail_z = _slab_rows(extz_ref, cb, tm + CONV_HALO - CB, CB)
        conv_o_ref[:, _slab(cb)] = tail_z
        extz_ref[cb, CONV_HALO - CB:CONV_HALO, :] = tail_z

    q = _dot(h, win_ref[:, 3 * D:4 * D])
    outs = []
    for hd in range(NH):
        cols = slice(hd * DH, (hd + 1) * DH)
        s = _dot_t(q[:, cols].astype(BF), k_ref[:, cols]) * (DH ** -0.5)
        outs.append(_dot(_softmax(s).astype(BF), v_ref[:, cols]))
    y_att = jnp.concatenate(outs, axis=-1)
    merged = merged + jax.nn.sigmoid(_dot(h, win_ref[:, 6 * D:7 * D])) * y_att

    xo_ref[...] = x + _dot(merged.astype(BF), wout_ref[...])


def _prompt_mix(x, layer, mix_norm, w_in, w_pool, pool_scale, w_dw, b_dw, ln_g, ln_b, w_pw,
                w_out, memk_b, memv_b):
    nc = SEQ // TM_MIX
    tok_spec = pl.BlockSpec((TM_MIX, D), lambda b, c: (b * nc + c, 0))
    mem_spec = pl.BlockSpec((None, NMEM, D), lambda b, c: (layer, b, 0))
    vec = _const_spec((1, D), layer)
    return pl.pallas_call(
        _prompt_mix_kernel,
        grid=(BATCH, nc),
        in_specs=[tok_spec, vec, _const_spec((D, 7 * D), layer),
                  _const_spec((len(WINDOWS), GD, GD), layer), vec, _const_spec((CK, D), layer),
                  vec, vec, vec, _const_spec((D, D), layer), _const_spec((D, D), layer),
                  mem_spec, mem_spec],
        out_specs=[tok_spec,
                   pl.BlockSpec((None, PB, D), lambda b, c: (b, 0, 0)),
                   pl.BlockSpec((None, CB, D), lambda b, c: (b, 0, 0))],
        out_shape=[jax.ShapeDtypeStruct((NT, D), F32),
                   jax.ShapeDtypeStruct((BATCH, PB, D), F32),
                   jax.ShapeDtypeStruct((BATCH, CB, D), F32)],
        scratch_shapes=[pltpu.VMEM((NSLAB, POOL_HALO + TM_MIX, LANES), F32),
                        pltpu.VMEM((NSLAB, CONV_HALO + TM_MIX, LANES), F32),
                        pltpu.VMEM((TM_MIX, D), F32)],
        input_output_aliases={0: 0},
        compiler_params=pltpu.CompilerParams(
            dimension_semantics=("arbitrary", "arbitrary"), vmem_limit_bytes=VMEM_LIMIT),
        name="prompt_mix",
    )(x, mix_norm, w_in, w_pool, pool_scale, w_dw, b_dw, ln_g, ln_b, w_pw, w_out,
      memk_b, memv_b)


def _sample_pre_kernel(x_ref, g_ref, win_ref, wpool_ref, ps_ref, wdw_ref, bdw_ref, lng_ref,
                       lnb_ref, wpw_ref, sp_ref, sc_ref,
                       part_ref, q_ref, gc_ref, pool_o_ref, conv_o_ref, y_ref):
    sb, t = SB_PRE, DEC_SEQ
    x = x_ref[...].reshape(t * sb, D)
    h = _rms(x, g_ref[...]).astype(BF)

    u3 = _dot(h, win_ref[:, 0:D]).reshape(t, sb, D)
    pool_o_ref[0:PB - t] = sp_ref[t:PB]
    pool_o_ref[PB - t:PB] = u3
    pos = PAST_LEN + lax.broadcasted_iota(jnp.int32, (t, 1, 1), 0)
    ys = []
    for gi, w in enumerate(WINDOWS):
        cols = slice(gi * GD, (gi + 1) * GD)
        s = u3[:, :, cols]
        for j in range(1, w):
            prev = sp_ref[PB - j:PB - j + min(j, t), :, cols]
            s = s + (prev if j >= t else jnp.concatenate([prev, u3[0:t - j, :, cols]], axis=0))
        inv_cnt = 1.0 / jnp.minimum(pos + 1, w).astype(F32)
        pooled = (s * inv_cnt - u3[:, :, cols]).reshape(t * sb, GD).astype(BF)
        ys.append(_dot(pooled, wpool_ref[gi]))
    y_pool = jnp.concatenate(ys, axis=-1) * ps_ref[...]
    part = jax.nn.sigmoid(_dot(h, win_ref[:, 4 * D:5 * D])) * y_pool

    z3 = (_dot(h, win_ref[:, D:2 * D]) *
          jax.nn.sigmoid(_dot(h, win_ref[:, 2 * D:3 * D]))).reshape(t, sb, D)
    conv_o_ref[0:CB - t] = sc_ref[t:CB]
    conv_o_ref[CB - t:CB] = z3
    for cb in range(NSLAB):
        for s0 in range(0, sb, SUBLANES):
            rows = slice(s0, s0 + SUBLANES)
            acc = None
            for k in range(CK):
                lo = sc_ref[k:t, rows, _slab(cb)] if k < t else None
                hi = conv_o_ref[max(k, t) - t:k, rows, _slab(cb)] if k > 0 else None
                win = lo if hi is None else hi if lo is None else jnp.concatenate([lo, hi], 0)
                term = win * wdw_ref[k:k + 1, _slab(cb)]
                acc = term if acc is None else acc + term
            y_ref[:, rows, _slab(cb)] = acc
    y_conv = _conv_ln_pw(y_ref[...].reshape(t * sb, D), bdw_ref, lng_ref, lnb_ref, wpw_ref)
    part = part + jax.nn.sigmoid(_dot(h, win_ref[:, 5 * D:6 * D])) * y_conv
    part_ref[...] = part.reshape(t, sb, D)

    q_ref[...] = _dot(h, win_ref[:, 3 * D:4 * D]).reshape(t, sb, D)
    gc_ref[...] = jax.nn.sigmoid(_dot(h, win_ref[:, 6 * D:7 * D])).reshape(t, sb, D)


def _sample_pre(xs, layer, mix_norm, w_in, w_pool, pool_scale, w_dw, b_dw, ln_g, ln_b, w_pw,
                state_pool_t, state_conv_t):
    vec = _const_spec((1, D), layer)
    tok = pl.BlockSpec((DEC_SEQ, SB_PRE, D), lambda i: (0, i, 0))
    tok_sds = jax.ShapeDtypeStruct((DEC_SEQ, DEC_BATCH, D), F32)
    return pl.pallas_call(
        _sample_pre_kernel,
        grid=(DEC_BATCH // SB_PRE,),
        in_specs=[tok, vec, _const_spec((D, 7 * D), layer),
                  _const_spec((len(WINDOWS), GD, GD), layer),
                  vec, _const_spec((CK, D), layer), vec, vec, vec, _const_spec((D, D), layer),
                  pl.BlockSpec((None, PB, SB_PRE, D), lambda i: (layer, 0, i, 0)),
                  pl.BlockSpec((None, CB, SB_PRE, D), lambda i: (layer, 0, i, 0))],
        out_specs=[tok, tok, tok,
                   pl.BlockSpec((PB, SB_PRE, D), lambda i: (0, i, 0)),
                   pl.BlockSpec((CB, SB_PRE, D), lambda i: (0, i, 0))],
        out_shape=[tok_sds, tok_sds, tok_sds,
                   jax.ShapeDtypeStruct((PB, DEC_BATCH, D), F32),
                   jax.ShapeDtypeStruct((CB, DEC_BATCH, D), F32)],
        scratch_shapes=[pltpu.VMEM((DEC_SEQ, SB_PRE, D), F32)],
        compiler_params=pltpu.CompilerParams(
            dimension_semantics=("arbitrary",), vmem_limit_bytes=VMEM_LIMIT),
        name="sample_pre",
    )(xs, mix_norm, w_in, w_pool, pool_scale, w_dw, b_dw, ln_g, ln_b, w_pw,
      state_pool_t, state_conv_t)


def _sample_att_kernel(x_ref, part_ref, q_ref, gc_ref, k_ref, v_ref, wout_ref, xo_ref,
                       qs_ref, ys_ref):
    sb, t = SB_ATT, DEC_SEQ
    half = NH * t
    for cb in range(NSLAB):
        qs_ref[cb] = q_ref[:, :, _slab(cb)].reshape(t * sb, LANES)

    row_head = lax.broadcasted_iota(jnp.int32, (half, KV_ROWS), 0) // t
    lane_slot = lax.broadcasted_iota(jnp.int32, (half, KV_ROWS), 1) % SUBLANES
    own = lane_slot == row_head

    for s in range(sb):
        qst = jnp.concatenate(
            [qs_ref[pl.ds(2 * hd + j, 1, stride=2), pl.ds(s, t, stride=sb), :].reshape(t, LANES)
             for j in range(2) for hd in range(NH)], axis=0).astype(BF)
        prod = _dot_t(qst, k_ref[s].astype(BF))
        sc = prod[0:half] + pltpu.roll(prod[half:2 * half], KV_ROWS - NH, axis=1)
        sc = jnp.where(own, sc * (DH ** -0.5), -jnp.inf)
        p = _softmax(sc)
        pst = jnp.concatenate([p, pltpu.roll(p, NH, axis=1)], axis=0).astype(BF)
        o = _dot(pst, v_ref[s].astype(BF))
        for j in range(2):
            for hd in range(NH):
                r0 = (j * NH + hd) * t
                ys_ref[pl.ds(2 * hd + j, 1, stride=2), pl.ds(s, t, stride=sb), :] = (
                    o[r0:r0 + t].reshape(1, t, LANES))

    y_att = jnp.concatenate([ys_ref[cb] for cb in range(NSLAB)], axis=-1)
    merged = part_ref[...].reshape(t * sb, D) + gc_ref[...].reshape(t * sb, D) * y_att
    out = x_ref[...].reshape(t * sb, D) + _dot(merged.astype(BF), wout_ref[...])
    xo_ref[...] = out.reshape(t, sb, D)


def _sample_att(xs, layer, part, q, gc, cache_k, cache_v, w_out):
    tok = pl.BlockSpec((DEC_SEQ, SB_ATT, D), lambda i: (0, i, 0))
    kv = pl.BlockSpec((None, SB_ATT, KV_ROWS, LANES), lambda i: (layer, i, 0, 0))
    return pl.pallas_call(
        _sample_att_kernel,
        grid=(DEC_BATCH // SB_ATT,),
        in_specs=[tok, tok, tok, tok, kv, kv, _const_spec((D, D), layer)],
        out_specs=tok,
        out_shape=jax.ShapeDtypeStruct((DEC_SEQ, DEC_BATCH, D), F32),
        scratch_shapes=[pltpu.VMEM((NSLAB, DEC_SEQ * SB_ATT, LANES), F32),
                        pltpu.VMEM((NSLAB, DEC_SEQ * SB_ATT, LANES), F32)],
        compiler_params=pltpu.CompilerParams(
            dimension_semantics=("arbitrary",), vmem_limit_bytes=VMEM_LIMIT),
        name="sample_att",
    )(xs, part, q, gc, cache_k, cache_v, w_out)


def _cache_rows(c):
    c = c.reshape(DEPTH, DEC_BATCH, NMEM, NH, DH // LANES, LANES)
    return c.transpose(0, 1, 2, 4, 3, 5).reshape(DEPTH, DEC_BATCH, KV_ROWS, LANES)


def kernel(x_prompt, x_sample, mem_prompt, state_pool, state_conv, cache_mem_k, cache_mem_v,
           ffn1_norm, ffn1_w_gu, ffn1_w_down, mix_norm, w_in, mem_norm, w_mem_kv, w_pool,
           pool_scale, w_dw, b_dw, conv_ln_g, conv_ln_b, w_pw, w_out, ffn2_norm, ffn2_w_gu,
           ffn2_w_down, final_norm):
    assert x_prompt.shape == (BATCH, SEQ, D) and x_sample.shape == (DEC_BATCH, DEC_SEQ, D)
    x = jnp.concatenate([x_prompt.reshape(NP, D),
                         x_sample.transpose(1, 0, 2).reshape(NS, D)], axis=0)

    def vec(a):
        return a.reshape(DEPTH, 1, D)

    ffn1_norm, mix_norm, mem_norm, pool_scale, b_dw, conv_ln_g, conv_ln_b, ffn2_norm = map(
        vec, (ffn1_norm, mix_norm, mem_norm, pool_scale, b_dw, conv_ln_g, conv_ln_b, ffn2_norm))
    ffn1_w_gu, ffn1_w_down, w_in, w_mem_kv, w_pool, w_pw, w_out, ffn2_w_gu, ffn2_w_down = (
        a.astype(BF) for a in (ffn1_w_gu, ffn1_w_down, w_in, w_mem_kv, w_pool, w_pw, w_out,
                               ffn2_w_gu, ffn2_w_down))
    cache_k = _cache_rows(cache_mem_k)
    cache_v = _cache_rows(cache_mem_v)
    state_pool_t = state_pool.transpose(0, 2, 1, 3)
    state_conv_t = state_conv.transpose(0, 2, 1, 3)

    memk, memv, memk_b, memv_b = _memory_kv(mem_prompt.reshape(BATCH * NMEM, D), mem_norm,
                                            w_mem_kv)

    pool_p, conv_p, pool_s, conv_s = [], [], [], []
    for l in range(DEPTH):
        x = _ffn(x, ffn1_norm, ffn1_w_gu, ffn1_w_down, l)
        mix_w = (mix_norm, w_in, w_pool, pool_scale, w_dw, b_dw, conv_ln_g, conv_ln_b, w_pw)
        xs = x[NP:].reshape(DEC_SEQ, DEC_BATCH, D)
        part, q, gc, spool, sconv = _sample_pre(xs, l, *mix_w, state_pool_t, state_conv_t)
        xs = _sample_att(xs, l, part, q, gc, cache_k, cache_v, w_out)
        x, ppool, pconv = _prompt_mix(x, l, *mix_w, w_out, memk_b, memv_b)
        x = lax.dynamic_update_slice(x, xs.reshape(NS, D), (NP, 0))
        pool_p.append(ppool)
        conv_p.append(pconv)
        pool_s.append(spool)
        conv_s.append(sconv)
        x = _ffn(x, ffn2_norm, ffn2_w_gu, ffn2_w_down, l,
                 final_norm=final_norm.reshape(1, D) if l == DEPTH - 1 else None)

    kv_shape = (DEPTH, BATCH, NMEM, NH, DH)
    y_sample = x[NP:].reshape(DEC_SEQ, DEC_BATCH, D).transpose(1, 0, 2)
    return (x[:NP].reshape(BATCH, SEQ, D), y_sample,
            jnp.stack(pool_p), jnp.stack(conv_p), memk.reshape(kv_shape), memv.reshape(kv_shape),
            jnp.stack(pool_s).transpose(0, 2, 1, 3), jnp.stack(conv_s).transpose(0, 2, 1, 3))
```

```python
import functools

import jax
import jax.numpy as jnp
from jax import lax
from jax.experimental import pallas as pl
from jax.experimental.pallas import tpu as pltpu

F32 = jnp.float32
BF = jnp.bfloat16

D = 1024
DFF = 2816
DEPTH = 4
BATCH, SEQ = 8, 2048
DEC_BATCH, DEC_SEQ = 128, 8
PAST_LEN = 16384
WINDOWS = (2, 4, 8, 16)
GD = D // len(WINDOWS)
PB = max(WINDOWS) - 1
CK = 31
CB = CK - 1
N_GATES = 3
NMEM = 256
NH = 4
DH = D // NH
EPS = 1e-6
NP = BATCH * SEQ
NS = DEC_BATCH * DEC_SEQ
NT = NP + NS

SUBLANES = 8
LANES = 128
NSLAB = D // LANES
POOL_HALO = 16
CONV_HALO = 32
VMEM_LIMIT = 58 * 1024 * 1024

TM_FFN = 512
FFN_CHUNKS = (512, 512, 512, 512, 512, 256)
TM_MIX = 256
CONV_ROWS = 64
SB_PRE = 16
SB_ATT = 8
KV_ROWS = NMEM * NH * DH // LANES


def _dot(a, b):
    return jnp.dot(a, b, preferred_element_type=F32)


def _dot_t(a, b):
    return lax.dot_general(a, b, (((1,), (1,)), ((), ())), preferred_element_type=F32)


def _rms(x, g):
    return x * lax.rsqrt(jnp.mean(x * x, axis=-1, keepdims=True) + EPS) * g


def _silu(x):
    return x * jax.nn.sigmoid(x)


def _layernorm(y, g, b):
    mu = jnp.mean(y, axis=-1, keepdims=True)
    yc = y - mu
    var = jnp.mean(yc * yc, axis=-1, keepdims=True)
    return yc * lax.rsqrt(var + EPS) * g + b


def _softmax(s):
    e = jnp.exp(s - jnp.max(s, axis=-1, keepdims=True))
    return e / jnp.sum(e, axis=-1, keepdims=True)


def _conv_ln_pw(y, bdw_ref, lng_ref, lnb_ref, wpw_ref):
    yn = _layernorm(y + bdw_ref[...], lng_ref[...], lnb_ref[...])
    return _dot(_silu(yn).astype(BF), wpw_ref[...])


def _slab(cb):
    return slice(cb * LANES, (cb + 1) * LANES)


def _slab_rows(ref, cb, start, n):
    return ref[pl.ds(cb, 1, stride=2), pl.ds(start, n), :].reshape(n, LANES)


def _const_spec(shape, layer=None):
    if layer is None:
        return pl.BlockSpec(shape, lambda *_: (0,) * len(shape),
                            pipeline_mode=pl.Buffered(1))
    return pl.BlockSpec((None,) + shape, lambda *_: (layer,) + (0,) * len(shape),
                        pipeline_mode=pl.Buffered(1))


def _memkv_kernel(m_ref, g_ref, w_ref, k_ref, v_ref, kb_ref, vb_ref):
    h = _rms(m_ref[...], g_ref[...]).astype(BF)
    for c0, o_ref, ob_ref in ((0, k_ref, kb_ref), (D, v_ref, vb_ref)):
        kv = _dot(h, w_ref[:, c0:c0 + D])
        ob_ref[...] = kv.astype(BF)
        for j in range(DH // LANES):
            for hd in range(NH):
                o_ref[pl.ds(NH * j + hd, NMEM, stride=KV_ROWS // NMEM), :] = (
                    kv[:, hd * DH + j * LANES:hd * DH + (j + 1) * LANES])


def _memory_kv(mem, mem_norm, w_kv):
    raw = jax.ShapeDtypeStruct((DEPTH, BATCH, KV_ROWS, LANES), F32)
    mat = jax.ShapeDtypeStruct((DEPTH, BATCH * NMEM, D), BF)
    raw_spec = pl.BlockSpec((None, None, KV_ROWS, LANES), lambda l, b: (l, b, 0, 0))
    mat_spec = pl.BlockSpec((None, NMEM, D), lambda l, b: (l, b, 0))
    return pl.pallas_call(
        _memkv_kernel,
        grid=(DEPTH, BATCH),
        in_specs=[pl.BlockSpec((NMEM, D), lambda l, b: (b, 0)),
                  pl.BlockSpec((None, 1, D), lambda l, b: (l, 0, 0)),
                  pl.BlockSpec((None, D, 2 * D), lambda l, b: (l, 0, 0))],
        out_specs=[raw_spec, raw_spec, mat_spec, mat_spec],
        out_shape=[raw, raw, mat, mat],
        compiler_params=pltpu.CompilerParams(
            dimension_semantics=("arbitrary", "arbitrary"), vmem_limit_bytes=VMEM_LIMIT),
        name="memory_kv",
    )(mem, mem_norm, w_kv)


def _cache_rows_inverse(c):
    c = c.reshape(DEPTH, BATCH, NMEM, DH // LANES, NH, LANES)
    return c.transpose(0, 1, 2, 4, 3, 5).reshape(DEPTH, BATCH, NMEM, NH, DH)


NPB = NP // TM_FFN


def _ffn_kernel(*refs, first, final):
    refs = list(refs)
    a_ref = refs.pop()
    i = pl.program_id(0)
    if first:
        xp_ref, xs_ref = refs.pop(0), refs.pop(0)
        x = jnp.where(i < NPB, xp_ref[...], xs_ref[...])
    else:
        x = refs.pop(0)[...]
    g_ref, wgu_ref, wd_ref = refs[:3]
    h = _rms(x, g_ref[...]).astype(BF)
    c0 = 0
    for width in FFN_CHUNKS:
        gate = _dot(h, wgu_ref[:, c0:c0 + width])
        up = _dot(h, wgu_ref[:, DFF + c0:DFF + c0 + width])
        a_ref[:, c0:c0 + width] = (_silu(gate) * up).astype(BF)
        c0 += width
    y = x + 0.5 * _dot(a_ref[...], wd_ref[...])
    if not final:
        refs[3][...] = y
        return
    fg_ref, yp_ref, ys_ref = refs[3:]
    y = _rms(y, fg_ref[...])

    @pl.when(i < NPB)
    def _():
        yp_ref[...] = y

    @pl.when(i >= NPB)
    def _():
        ys_ref[...] = y


def _ffn(xs, norm, w_gu, w_down, layer, final_norm=None):
    first, final = len(xs) == 2, final_norm is not None
    tok_spec = pl.BlockSpec((TM_FFN, D), lambda i: (i, 0))
    prompt_spec = pl.BlockSpec((TM_FFN, D), lambda i: (jnp.minimum(i, NPB - 1), 0))
    sample_spec = pl.BlockSpec((TM_FFN, D), lambda i: (jnp.maximum(i - NPB, 0), 0))
    in_specs = ([prompt_spec, sample_spec] if first else [tok_spec]) + [
        _const_spec((1, D), layer), _const_spec((D, 2 * DFF), layer),
        _const_spec((DFF, D), layer)]
    args = list(xs) + [norm, w_gu, w_down]
    if final:
        in_specs.append(_const_spec((1, D)))
        args.append(final_norm)
        out_specs = [prompt_spec, sample_spec]
        out_shape = [jax.ShapeDtypeStruct((NP, D), F32), jax.ShapeDtypeStruct((NS, D), F32)]
    else:
        out_specs = tok_spec
        out_shape = jax.ShapeDtypeStruct((NT, D), F32)
    return pl.pallas_call(
        functools.partial(_ffn_kernel, first=first, final=final),
        grid=(NT // TM_FFN,),
        in_specs=in_specs,
        out_specs=out_specs,
        out_shape=out_shape,
        scratch_shapes=[pltpu.VMEM((TM_FFN, DFF), BF)],
        compiler_params=pltpu.CompilerParams(
            dimension_semantics=("arbitrary",), vmem_limit_bytes=VMEM_LIMIT),
        name="ffn_first" if first else "ffn_final" if final else "ffn",
    )(*args)


def _prompt_mix_kernel(x_ref, g_ref, win_ref, wpool_ref, ps_ref, wdw_ref, bdw_ref, lng_ref,
                       lnb_ref, wpw_ref, wout_ref, k_ref, v_ref,
                       xo_ref, pool_o_ref, conv_o_ref,
                       extu_ref, extz_ref, y_ref, h_ref, q_ref, gate_ref):
    tm = TM_MIX
    c = pl.program_id(1)
    ngroups = len(WINDOWS)

    @pl.when(c == 0)
    def _():
        extu_ref[:, 0:POOL_HALO, :] = jnp.zeros((NSLAB, POOL_HALO, LANES), F32)
        extz_ref[:, 0:CONV_HALO, :] = jnp.zeros((NSLAB, CONV_HALO, LANES), F32)

    x = x_ref[...]
    h_ref[...] = _rms(x, g_ref[...]).astype(BF)

    slabs = GD // LANES
    first = CONV_HALO - CB

    def proj(kind, gi):
        col = kind * D + gi * GD
        return _dot(h_ref[...], win_ref[:, col:col + GD])

    for gi in range(ngroups):
        z = proj(1, gi) * jax.nn.sigmoid(proj(2, gi))
        for s in range(slabs):
            extz_ref[gi * slabs + s, CONV_HALO:CONV_HALO + tm, :] = z[:, _slab(s)]
        for s in range(slabs):
            cb = gi * slabs + s
            for r0 in range(0, tm, CONV_ROWS):
                acc = None
                for k in range(CK):
                    term = (_slab_rows(extz_ref, cb, first + r0 + k, CONV_ROWS) *
                            wdw_ref[k:k + 1, _slab(cb)])
                    acc = term if acc is None else acc + term
                y_ref[cb, r0:r0 + CONV_ROWS, :] = acc
        u = proj(0, gi)
        for s in range(slabs):
            extu_ref[gi * slabs + s, POOL_HALO:POOL_HALO + tm, :] = u[:, _slab(s)]
        q_ref[gi] = proj(3, gi).astype(BF)
        for n in range(N_GATES):
            gate_ref[n, gi] = jax.nn.sigmoid(proj(4 + n, gi))
    for cb in range(NSLAB):
        tail_z = _slab_rows(extz_ref, cb, tm + CONV_HALO - CB, CB)
        conv_o_ref[:, _slab(cb)] = tail_z
        extz_ref[cb, CONV_HALO - CB:CONV_HALO, :] = tail_z

    y_conv = _conv_ln_pw(jnp.concatenate([y_ref[cb] for cb in range(NSLAB)], axis=-1),
                         bdw_ref, lng_ref, lnb_ref, wpw_ref)

    pos = c * tm + lax.broadcasted_iota(jnp.int32, (tm, 1), 0)
    merged = []
    for gi, w in enumerate(WINDOWS):
        cols = slice(gi * GD, (gi + 1) * GD)
        inv_cnt = 1.0 / jnp.minimum(pos + 1, w).astype(F32)
        parts = []
        for cb in range(gi * GD // LANES, (gi + 1) * GD // LANES):
            u = extu_ref[cb, POOL_HALO:POOL_HALO + tm, :]
            s = u
            for j in range(1, w):
                s = s + _slab_rows(extu_ref, cb, POOL_HALO - j, tm)
            parts.append(s * inv_cnt - u)
        pooled = jnp.concatenate(parts, axis=-1).astype(BF)
        y_pool = _dot(pooled, wpool_ref[gi]) * ps_ref[:, cols]
        s = _dot_t(q_ref[gi], k_ref[:, cols]) * (DH ** -0.5)
        y_att = _dot(_softmax(s).astype(BF), v_ref[:, cols])
        merged.append(gate_ref[0, gi] * y_pool + gate_ref[1, gi] * y_conv[:, cols] +
                      gate_ref[2, gi] * y_att)
    for cb in range(NSLAB):
        tail_u = _slab_rows(extu_ref, cb, tm + POOL_HALO - PB, PB)
        pool_o_ref[:, _slab(cb)] = tail_u
        extu_ref[cb, POOL_HALO - PB:POOL_HALO, :] = tail_u

    xo_ref[...] = x + _dot(jnp.concatenate(merged, axis=-1).astype(BF), wout_ref[...])


def _prompt_mix(x, layer, mix_norm, w_in, w_pool, pool_scale, w_dw, b_dw, ln_g, ln_b, w_pw,
                w_out, memk_b, memv_b):
    nc = SEQ // TM_MIX
    tok_spec = pl.BlockSpec((TM_MIX, D), lambda b, c: (b * nc + c, 0))
    mem_spec = pl.BlockSpec((None, NMEM, D), lambda b, c: (layer, b, 0))
    vec = _const_spec((1, D), layer)
    return pl.pallas_call(
        _prompt_mix_kernel,
        grid=(BATCH, nc),
        in_specs=[tok_spec, vec, _const_spec((D, 7 * D), layer),
                  _const_spec((len(WINDOWS), GD, GD), layer), vec, _const_spec((CK, D), layer),
                  vec, vec, vec, _const_spec((D, D), layer), _const_spec((D, D), layer),
                  mem_spec, mem_spec],
        out_specs=[tok_spec,
                   pl.BlockSpec((None, PB, D), lambda b, c: (b, 0, 0)),
                   pl.BlockSpec((None, CB, D), lambda b, c: (b, 0, 0))],
        out_shape=[jax.ShapeDtypeStruct((NT, D), F32),
                   jax.ShapeDtypeStruct((BATCH, PB, D), F32),
                   jax.ShapeDtypeStruct((BATCH, CB, D), F32)],
        scratch_shapes=[pltpu.VMEM((NSLAB, POOL_HALO + TM_MIX, LANES), F32),
                        pltpu.VMEM((NSLAB, CONV_HALO + TM_MIX, LANES), F32),
                        pltpu.VMEM((NSLAB, TM_MIX, LANES), F32),
                        pltpu.VMEM((TM_MIX, D), BF),
                        pltpu.VMEM((len(WINDOWS), TM_MIX, GD), BF),
                        pltpu.VMEM((N_GATES, len(WINDOWS), TM_MIX, GD), F32)],
        input_output_aliases={0: 0},
        compiler_params=pltpu.CompilerParams(
            dimension_semantics=("arbitrary", "arbitrary"), vmem_limit_bytes=VMEM_LIMIT),
        name="prompt_mix",
    )(x, mix_norm, w_in, w_pool, pool_scale, w_dw, b_dw, ln_g, ln_b, w_pw, w_out,
      memk_b, memv_b)


def _sample_pre_kernel(x_ref, g_ref, win_ref, wpool_ref, ps_ref, wdw_ref, bdw_ref, lng_ref,
                       lnb_ref, wpw_ref, sp_ref, sc_ref,
                       part_ref, q_ref, gc_ref, pool_o_ref, conv_o_ref, y_ref):
    sb, t = SB_PRE, DEC_SEQ
    x = x_ref[...].reshape(t * sb, D)
    h = _rms(x, g_ref[...]).astype(BF)

    u3 = _dot(h, win_ref[:, 0:D]).reshape(t, sb, D)
    pool_o_ref[0:PB - t] = sp_ref[t:PB]
    pool_o_ref[PB - t:PB] = u3
    pos = PAST_LEN + lax.broadcasted_iota(jnp.int32, (t, 1, 1), 0)
    ys = []
    for gi, w in enumerate(WINDOWS):
        cols = slice(gi * GD, (gi + 1) * GD)
        s = u3[:, :, cols]
        for j in range(1, w):
            prev = sp_ref[PB - j:PB - j + min(j, t), :, cols]
            s = s + (prev if j >= t else jnp.concatenate([prev, u3[0:t - j, :, cols]], axis=0))
        inv_cnt = 1.0 / jnp.minimum(pos + 1, w).astype(F32)
        pooled = (s * inv_cnt - u3[:, :, cols]).reshape(t * sb, GD).astype(BF)
        ys.append(_dot(pooled, wpool_ref[gi]))
    y_pool = jnp.concatenate(ys, axis=-1) * ps_ref[...]
    part = jax.nn.sigmoid(_dot(h, win_ref[:, 4 * D:5 * D])) * y_pool

    z3 = (_dot(h, win_ref[:, D:2 * D]) *
          jax.nn.sigmoid(_dot(h, win_ref[:, 2 * D:3 * D]))).reshape(t, sb, D)
    conv_o_ref[0:CB - t] = sc_ref[t:CB]
    conv_o_ref[CB - t:CB] = z3
    for cb in range(NSLAB):
        for s0 in range(0, sb, SUBLANES):
            rows = slice(s0, s0 + SUBLANES)
            acc = None
            for k in range(CK):
                lo = sc_ref[k:t, rows, _slab(cb)] if k < t else None
                hi = conv_o_ref[max(k, t) - t:k, rows, _slab(cb)] if k > 0 else None
                win = lo if hi is None else hi if lo is None else jnp.concatenate([lo, hi], 0)
                term = win * wdw_ref[k:k + 1, _slab(cb)]
                acc = term if acc is None else acc + term
            y_ref[:, rows, _slab(cb)] = acc
    y_conv = _conv_ln_pw(y_ref[...].reshape(t * sb, D), bdw_ref, lng_ref, lnb_ref, wpw_ref)
    part = part + jax.nn.sigmoid(_dot(h, win_ref[:, 5 * D:6 * D])) * y_conv
    part_ref[...] = part.reshape(t, sb, D)

    q_ref[...] = _dot(h, win_ref[:, 3 * D:4 * D]).reshape(t, sb, D)
    gc_ref[...] = jax.nn.sigmoid(_dot(h, win_ref[:, 6 * D:7 * D])).reshape(t, sb, D)


def _sample_pre(xs, layer, mix_norm, w_in, w_pool, pool_scale, w_dw, b_dw, ln_g, ln_b, w_pw,
                state_pool_t, state_conv_t):
    vec = _const_spec((1, D), layer)
    tok = pl.BlockSpec((DEC_SEQ, SB_PRE, D), lambda i: (0, i, 0))
    tok_sds = jax.ShapeDtypeStruct((DEC_SEQ, DEC_BATCH, D), F32)
    return pl.pallas_call(
        _sample_pre_kernel,
        grid=(DEC_BATCH // SB_PRE,),
        in_specs=[tok, vec, _const_spec((D, 7 * D), layer),
                  _const_spec((len(WINDOWS), GD, GD), layer),
                  vec, _const_spec((CK, D), layer), vec, vec, vec, _const_spec((D, D), layer),
                  pl.BlockSpec((None, PB, SB_PRE, D), lambda i: (layer, 0, i, 0)),
                  pl.BlockSpec((None, CB, SB_PRE, D), lambda i: (layer, 0, i, 0))],
        out_specs=[tok, tok, tok,
                   pl.BlockSpec((PB, SB_PRE, D), lambda i: (0, i, 0)),
                   pl.BlockSpec((CB, SB_PRE, D), lambda i: (0, i, 0))],
        out_shape=[tok_sds, tok_sds, tok_sds,
                   jax.ShapeDtypeStruct((PB, DEC_BATCH, D), F32),
                   jax.ShapeDtypeStruct((CB, DEC_BATCH, D), F32)],
        scratch_shapes=[pltpu.VMEM((DEC_SEQ, SB_PRE, D), F32)],
        compiler_params=pltpu.CompilerParams(
            dimension_semantics=("arbitrary",), vmem_limit_bytes=VMEM_LIMIT),
        name="sample_pre",
    )(xs, mix_norm, w_in, w_pool, pool_scale, w_dw, b_dw, ln_g, ln_b, w_pw,
      state_pool_t, state_conv_t)


def _sample_att_kernel(x_ref, part_ref, q_ref, gc_ref, k_ref, v_ref, wout_ref, xo_ref,
                       qs_ref, ys_ref):
    sb, t = SB_ATT, DEC_SEQ
    half = NH * t
    for cb in range(NSLAB):
        qs_ref[cb] = q_ref[:, :, _slab(cb)].reshape(t * sb, LANES)

    row_head = lax.broadcasted_iota(jnp.int32, (half, KV_ROWS), 0) // t
    lane_slot = lax.broadcasted_iota(jnp.int32, (half, KV_ROWS), 1) % SUBLANES
    own = lane_slot == row_head

    for s in range(sb):
        qst = jnp.concatenate(
            [qs_ref[pl.ds(2 * hd + j, 1, stride=2), pl.ds(s, t, stride=sb), :].reshape(t, LANES)
             for j in range(2) for hd in range(NH)], axis=0).astype(BF)
        prod = _dot_t(qst, k_ref[s].astype(BF))
        sc = prod[0:half] + pltpu.roll(prod[half:2 * half], KV_ROWS - NH, axis=1)
        sc = jnp.where(own, sc * (DH ** -0.5), -jnp.inf)
        p = _softmax(sc)
        pst = jnp.concatenate([p, pltpu.roll(p, NH, axis=1)], axis=0).astype(BF)
        o = _dot(pst, v_ref[s].astype(BF))
        for j in range(2):
            for hd in range(NH):
                r0 = (j * NH + hd) * t
                ys_ref[pl.ds(2 * hd + j, 1, stride=2), pl.ds(s, t, stride=sb), :] = (
                    o[r0:r0 + t].reshape(1, t, LANES))

    y_att = jnp.concatenate([ys_ref[cb] for cb in range(NSLAB)], axis=-1)
    merged = part_ref[...].reshape(t * sb, D) + gc_ref[...].reshape(t * sb, D) * y_att
    out = x_ref[...].reshape(t * sb, D) + _dot(merged.astype(BF), wout_ref[...])
    xo_ref[...] = out.reshape(t, sb, D)


def _sample_att(xs, layer, part, q, gc, cache_k, cache_v, w_out):
    tok = pl.BlockSpec((DEC_SEQ, SB_ATT, D), lambda i: (0, i, 0))
    kv = pl.BlockSpec((None, SB_ATT, KV_ROWS, LANES), lambda i: (layer, i, 0, 0))
    return pl.pallas_call(
        _sample_att_kernel,
        grid=(DEC_BATCH // SB_ATT,),
        in_specs=[tok, tok, tok, tok, kv, kv, _const_spec((D, D), layer)],
        out_specs=tok,
        out_shape=jax.ShapeDtypeStruct((DEC_SEQ, DEC_BATCH, D), F32),
        scratch_shapes=[pltpu.VMEM((NSLAB, DEC_SEQ * SB_ATT, LANES), F32),
                        pltpu.VMEM((NSLAB, DEC_SEQ * SB_ATT, LANES), F32)],
        compiler_params=pltpu.CompilerParams(
            dimension_semantics=("arbitrary",), vmem_limit_bytes=VMEM_LIMIT),
        name="sample_att",
    )(xs, part, q, gc, cache_k, cache_v, w_out)


def _cache_rows(c):
    c = c.reshape(DEPTH, DEC_BATCH, NMEM, NH, DH // LANES, LANES)
    return c.transpose(0, 1, 2, 4, 3, 5).reshape(DEPTH, DEC_BATCH, KV_ROWS, LANES)


def kernel(x_prompt, x_sample, mem_prompt, state_pool, state_conv, cache_mem_k, cache_mem_v,
           ffn1_norm, ffn1_w_gu, ffn1_w_down, mix_norm, w_in, mem_norm, w_mem_kv, w_pool,
           pool_scale, w_dw, b_dw, conv_ln_g, conv_ln_b, w_pw, w_out, ffn2_norm, ffn2_w_gu,
           ffn2_w_down, final_norm):
    assert x_prompt.shape == (BATCH, SEQ, D) and x_sample.shape == (DEC_BATCH, DEC_SEQ, D)
    x = [x_prompt.reshape(NP, D), x_sample.transpose(1, 0, 2).reshape(NS, D)]

    def vec(a):
        return a.reshape(DEPTH, 1, D)

    ffn1_norm, mix_norm, mem_norm, pool_scale, b_dw, conv_ln_g, conv_ln_b, ffn2_norm = map(
        vec, (ffn1_norm, mix_norm, mem_norm, pool_scale, b_dw, conv_ln_g, conv_ln_b, ffn2_norm))
    ffn1_w_gu, ffn1_w_down, w_in, w_mem_kv, w_pool, w_pw, w_out, ffn2_w_gu, ffn2_w_down = (
        a.astype(BF) for a in (ffn1_w_gu, ffn1_w_down, w_in, w_mem_kv, w_pool, w_pw, w_out,
                               ffn2_w_gu, ffn2_w_down))
    cache_k = _cache_rows(cache_mem_k)
    cache_v = _cache_rows(cache_mem_v)
    state_pool_t = state_pool.transpose(0, 2, 1, 3)
    state_conv_t = state_conv.transpose(0, 2, 1, 3)

    memk, memv, memk_b, memv_b = _memory_kv(mem_prompt.reshape(BATCH * NMEM, D), mem_norm,
                                            w_mem_kv)

    pool_p, conv_p, pool_s, conv_s = [], [], [], []
    for l in range(DEPTH):
        x = _ffn(x, ffn1_norm, ffn1_w_gu, ffn1_w_down, l)
        mix_w = (mix_norm, w_in, w_pool, pool_scale, w_dw, b_dw, conv_ln_g, conv_ln_b, w_pw)
        xs = x[NP:].reshape(DEC_SEQ, DEC_BATCH, D)
        part, q, gc, spool, sconv = _sample_pre(xs, l, *mix_w, state_pool_t, state_conv_t)
        xs = _sample_att(xs, l, part, q, gc, cache_k, cache_v, w_out)
        x, ppool, pconv = _prompt_mix(x, l, *mix_w, w_out, memk_b, memv_b)
        x = lax.dynamic_update_slice(x, xs.reshape(NS, D), (NP, 0))
        pool_p.append(ppool)
        conv_p.append(pconv)
        pool_s.append(spool)
        conv_s.append(sconv)
        x = _ffn([x], ffn2_norm, ffn2_w_gu, ffn2_w_down, l,
                 final_norm=final_norm.reshape(1, D) if l == DEPTH - 1 else None)
        if l < DEPTH - 1:
            x = [x]

    y_prompt, y_sample = x
    y_sample = y_sample.reshape(DEC_SEQ, DEC_BATCH, D).transpose(1, 0, 2)
    return (y_prompt.reshape(BATCH, SEQ, D), y_sample,
            jnp.stack(pool_p), jnp.stack(conv_p), _cache_rows_inverse(memk),
            _cache_rows_inverse(memv),
            jnp.stack(pool_s).transpose(0, 2, 1, 3), jnp.stack(conv_s).transpose(0, 2, 1, 3))
```

```python
import functools

import jax
import jax.numpy as jnp
from jax import lax
from jax.experimental import pallas as pl
from jax.experimental.pallas import tpu as pltpu

F32 = jnp.float32
BF = jnp.bfloat16

D = 1024
DFF = 2816
DEPTH = 4
BATCH, SEQ = 8, 2048
DEC_BATCH, DEC_SEQ = 128, 8
PAST_LEN = 16384
WINDOWS = (2, 4, 8, 16)
GD = D // len(WINDOWS)
PB = max(WINDOWS) - 1
CK = 31
CB = CK - 1
N_GATES = 3
NMEM = 256
NH = 4
DH = D // NH
EPS = 1e-6
NP = BATCH * SEQ
NS = DEC_BATCH * DEC_SEQ
NT = NP + NS

SUBLANES = 8
LANES = 128
NSLAB = D // LANES
POOL_HALO = 16
CONV_HALO = 32
VMEM_LIMIT = 58 * 1024 * 1024

TM_FFN = 512
FFN_CHUNKS = (512, 512, 512, 512, 512, 256)
TM_MIX = 512
CONV_ROWS = 64
SB_PRE = 16
SB_ATT = 8
KV_ROWS = NMEM * NH * DH // LANES


def _dot(a, b):
    return jnp.dot(a, b, preferred_element_type=F32)


def _dot_t(a, b):
    return lax.dot_general(a, b, (((1,), (1,)), ((), ())), preferred_element_type=F32)


def _rms(x, g):
    return x * lax.rsqrt(jnp.mean(x * x, axis=-1, keepdims=True) + EPS) * g


def _silu(x):
    return x * jax.nn.sigmoid(x)


def _layernorm(y, g, b):
    mu = jnp.mean(y, axis=-1, keepdims=True)
    yc = y - mu
    var = jnp.mean(yc * yc, axis=-1, keepdims=True)
    return yc * lax.rsqrt(var + EPS) * g + b


def _softmax(s):
    e = jnp.exp(s - jnp.max(s, axis=-1, keepdims=True))
    return e / jnp.sum(e, axis=-1, keepdims=True)


def _conv_ln_pw(y, bdw_ref, lng_ref, lnb_ref, wpw_ref):
    yn = _layernorm(y + bdw_ref[...], lng_ref[...], lnb_ref[...])
    return _dot(_silu(yn).astype(BF), wpw_ref[...])


def _slab(cb):
    return slice(cb * LANES, (cb + 1) * LANES)


def _slab_rows(ref, cb, start, n):
    return ref[pl.ds(cb, 1, stride=2), pl.ds(start, n), :].reshape(n, LANES)


def _const_spec(shape, layer=None):
    if layer is None:
        return pl.BlockSpec(shape, lambda *_: (0,) * len(shape),
                            pipeline_mode=pl.Buffered(1))
    return pl.BlockSpec((None,) + shape, lambda *_: (layer,) + (0,) * len(shape),
                        pipeline_mode=pl.Buffered(1))


def _memkv_kernel(m_ref, g_ref, w_ref, k_ref, v_ref, kb_ref, vb_ref):
    h = _rms(m_ref[...], g_ref[...]).astype(BF)
    for c0, o_ref, ob_ref in ((0, k_ref, kb_ref), (D, v_ref, vb_ref)):
        kv = _dot(h, w_ref[:, c0:c0 + D])
        ob_ref[...] = kv.astype(BF)
        for j in range(DH // LANES):
            for hd in range(NH):
                o_ref[pl.ds(NH * j + hd, NMEM, stride=KV_ROWS // NMEM), :] = (
                    kv[:, hd * DH + j * LANES:hd * DH + (j + 1) * LANES])


def _memory_kv(mem, mem_norm, w_kv):
    raw = jax.ShapeDtypeStruct((DEPTH, BATCH, KV_ROWS, LANES), F32)
    mat = jax.ShapeDtypeStruct((DEPTH, BATCH * NMEM, D), BF)
    raw_spec = pl.BlockSpec((None, None, KV_ROWS, LANES), lambda l, b: (l, b, 0, 0))
    mat_spec = pl.BlockSpec((None, NMEM, D), lambda l, b: (l, b, 0))
    return pl.pallas_call(
        _memkv_kernel,
        grid=(DEPTH, BATCH),
        in_specs=[pl.BlockSpec((NMEM, D), lambda l, b: (b, 0)),
                  pl.BlockSpec((None, 1, D), lambda l, b: (l, 0, 0)),
                  pl.BlockSpec((None, D, 2 * D), lambda l, b: (l, 0, 0))],
        out_specs=[raw_spec, raw_spec, mat_spec, mat_spec],
        out_shape=[raw, raw, mat, mat],
        compiler_params=pltpu.CompilerParams(
            dimension_semantics=("arbitrary", "arbitrary"), vmem_limit_bytes=VMEM_LIMIT),
        name="memory_kv",
    )(mem, mem_norm, w_kv)


def _cache_rows_inverse(c):
    c = c.reshape(DEPTH, BATCH, NMEM, DH // LANES, NH, LANES)
    return c.transpose(0, 1, 2, 4, 3, 5).reshape(DEPTH, BATCH, NMEM, NH, DH)


NPB = NP // TM_FFN


def _ffn_kernel(*refs, first, final):
    refs = list(refs)
    a_ref = refs.pop()
    i = pl.program_id(0)
    if first:
        xp_ref, xs_ref = refs.pop(0), refs.pop(0)
        x = jnp.where(i < NPB, xp_ref[...], xs_ref[...])
    else:
        x = refs.pop(0)[...]
    g_ref, wgu_ref, wd_ref = refs[:3]
    h = _rms(x, g_ref[...]).astype(BF)
    c0 = 0
    for width in FFN_CHUNKS:
        gate = _dot(h, wgu_ref[:, c0:c0 + width])
        up = _dot(h, wgu_ref[:, DFF + c0:DFF + c0 + width])
        a_ref[:, c0:c0 + width] = (_silu(gate) * up).astype(BF)
        c0 += width
    y = x + 0.5 * _dot(a_ref[...], wd_ref[...])
    if not final:
        refs[3][...] = y
        return
    fg_ref, yp_ref, ys_ref = refs[3:]
    y = _rms(y, fg_ref[...])

    @pl.when(i < NPB)
    def _():
        yp_ref[...] = y

    @pl.when(i >= NPB)
    def _():
        ys_ref[...] = y


def _ffn(xs, norm, w_gu, w_down, layer, final_norm=None):
    first, final = len(xs) == 2, final_norm is not None
    tok_spec = pl.BlockSpec((TM_FFN, D), lambda i: (i, 0))
    prompt_spec = pl.BlockSpec((TM_FFN, D), lambda i: (jnp.minimum(i, NPB - 1), 0))
    sample_spec = pl.BlockSpec((TM_FFN, D), lambda i: (jnp.maximum(i - NPB, 0), 0))
    in_specs = ([prompt_spec, sample_spec] if first else [tok_spec]) + [
        _const_spec((1, D), layer), _const_spec((D, 2 * DFF), layer),
        _const_spec((DFF, D), layer)]
    args = list(xs) + [norm, w_gu, w_down]
    if final:
        in_specs.append(_const_spec((1, D)))
        args.append(final_norm)
        out_specs = [prompt_spec, sample_spec]
        out_shape = [jax.ShapeDtypeStruct((NP, D), F32), jax.ShapeDtypeStruct((NS, D), F32)]
    else:
        out_specs = tok_spec
        out_shape = jax.ShapeDtypeStruct((NT, D), F32)
    return pl.pallas_call(
        functools.partial(_ffn_kernel, first=first, final=final),
        grid=(NT // TM_FFN,),
        in_specs=in_specs,
        out_specs=out_specs,
        out_shape=out_shape,
        scratch_shapes=[pltpu.VMEM((TM_FFN, DFF), BF)],
        compiler_params=pltpu.CompilerParams(
            dimension_semantics=("arbitrary",), vmem_limit_bytes=VMEM_LIMIT),
        name="ffn_first" if first else "ffn_final" if final else "ffn",
    )(*args)


def _prompt_mix_kernel(x_ref, g_ref, win_ref, wpool_ref, ps_ref, wdw_ref, bdw_ref, lng_ref,
                       lnb_ref, wpw_ref, wout_ref, k_ref, v_ref,
                       xo_ref, pool_o_ref, conv_o_ref,
                       extu_ref, extz_ref, y_ref, h_ref, q_ref, gate_ref):
    tm = TM_MIX
    c = pl.program_id(1)
    ngroups = len(WINDOWS)

    @pl.when(c == 0)
    def _():
        extu_ref[:, 0:POOL_HALO, :] = jnp.zeros((NSLAB, POOL_HALO, LANES), F32)
        extz_ref[:, 0:CONV_HALO, :] = jnp.zeros((NSLAB, CONV_HALO, LANES), F32)

    x = x_ref[...]
    h_ref[...] = _rms(x, g_ref[...]).astype(BF)

    slabs = GD // LANES
    first = CONV_HALO - CB

    def proj(kind, gi):
        col = kind * D + gi * GD
        return _dot(h_ref[...], win_ref[:, col:col + GD])

    for gi in range(ngroups):
        z = proj(1, gi) * jax.nn.sigmoid(proj(2, gi))
        for s in range(slabs):
            extz_ref[gi * slabs + s, CONV_HALO:CONV_HALO + tm, :] = z[:, _slab(s)]
        for s in range(slabs):
            cb = gi * slabs + s
            for r0 in range(0, tm, CONV_ROWS):
                acc = None
                for k in range(CK):
                    term = (_slab_rows(extz_ref, cb, first + r0 + k, CONV_ROWS) *
                            wdw_ref[k:k + 1, _slab(cb)])
                    acc = term if acc is None else acc + term
                y_ref[cb, r0:r0 + CONV_ROWS, :] = acc
        u = proj(0, gi)
        for s in range(slabs):
            extu_ref[gi * slabs + s, POOL_HALO:POOL_HALO + tm, :] = u[:, _slab(s)]
        q_ref[gi] = proj(3, gi).astype(BF)
        for n in range(N_GATES):
            gate_ref[n, gi] = jax.nn.sigmoid(proj(4 + n, gi))
    for cb in range(NSLAB):
        tail_z = _slab_rows(extz_ref, cb, tm + CONV_HALO - CB, CB)
        conv_o_ref[:, _slab(cb)] = tail_z
        extz_ref[cb, CONV_HALO - CB:CONV_HALO, :] = tail_z

    y_conv = _conv_ln_pw(jnp.concatenate([y_ref[cb] for cb in range(NSLAB)], axis=-1),
                         bdw_ref, lng_ref, lnb_ref, wpw_ref)

    pos = c * tm + lax.broadcasted_iota(jnp.int32, (tm, 1), 0)
    merged = []
    for gi, w in enumerate(WINDOWS):
        cols = slice(gi * GD, (gi + 1) * GD)
        inv_cnt = 1.0 / jnp.minimum(pos + 1, w).astype(F32)
        parts = []
        for cb in range(gi * GD // LANES, (gi + 1) * GD // LANES):
            u = extu_ref[cb, POOL_HALO:POOL_HALO + tm, :]
            s = u
            for j in range(1, w):
                s = s + _slab_rows(extu_ref, cb, POOL_HALO - j, tm)
            parts.append(s * inv_cnt - u)
        pooled = jnp.concatenate(parts, axis=-1).astype(BF)
        y_pool = _dot(pooled, wpool_ref[gi]) * ps_ref[:, cols]
        s = _dot_t(q_ref[gi], k_ref[:, cols]) * (DH ** -0.5)
        y_att = _dot(_softmax(s).astype(BF), v_ref[:, cols])
        merged.append(gate_ref[0, gi] * y_pool + gate_ref[1, gi] * y_conv[:, cols] +
                      gate_ref[2, gi] * y_att)
    for cb in range(NSLAB):
        tail_u = _slab_rows(extu_ref, cb, tm + POOL_HALO - PB, PB)
        pool_o_ref[:, _slab(cb)] = tail_u
        extu_ref[cb, POOL_HALO - PB:POOL_HALO, :] = tail_u

    xo_ref[...] = x + _dot(jnp.concatenate(merged, axis=-1).astype(BF), wout_ref[...])


def _prompt_mix(x, layer, mix_norm, w_in, w_pool, pool_scale, w_dw, b_dw, ln_g, ln_b, w_pw,
                w_out, memk_b, memv_b):
    nc = SEQ // TM_MIX
    tok_spec = pl.BlockSpec((TM_MIX, D), lambda b, c: (b * nc + c, 0))
    mem_spec = pl.BlockSpec((None, NMEM, D), lambda b, c: (layer, b, 0))
    vec = _const_spec((1, D), layer)
    return pl.pallas_call(
        _prompt_mix_kernel,
        grid=(BATCH, nc),
        in_specs=[tok_spec, vec, _const_spec((D, 7 * D), layer),
                  _const_spec((len(WINDOWS), GD, GD), layer), vec, _const_spec((CK, D), layer),
                  vec, vec, vec, _const_spec((D, D), layer), _const_spec((D, D), layer),
                  mem_spec, mem_spec],
        out_specs=[tok_spec,
                   pl.BlockSpec((None, PB, D), lambda b, c: (b, 0, 0)),
                   pl.BlockSpec((None, CB, D), lambda b, c: (b, 0, 0))],
        out_shape=[jax.ShapeDtypeStruct((NT, D), F32),
                   jax.ShapeDtypeStruct((BATCH, PB, D), F32),
                   jax.ShapeDtypeStruct((BATCH, CB, D), F32)],
        scratch_shapes=[pltpu.VMEM((NSLAB, POOL_HALO + TM_MIX, LANES), F32),
                        pltpu.VMEM((NSLAB, CONV_HALO + TM_MIX, LANES), F32),
                        pltpu.VMEM((NSLAB, TM_MIX, LANES), F32),
                        pltpu.VMEM((TM_MIX, D), BF),
                        pltpu.VMEM((len(WINDOWS), TM_MIX, GD), BF),
                        pltpu.VMEM((N_GATES, len(WINDOWS), TM_MIX, GD), F32)],
        input_output_aliases={0: 0},
        compiler_params=pltpu.CompilerParams(
            dimension_semantics=("arbitrary", "arbitrary"), vmem_limit_bytes=VMEM_LIMIT),
        name="prompt_mix",
    )(x, mix_norm, w_in, w_pool, pool_scale, w_dw, b_dw, ln_g, ln_b, w_pw, w_out,
      memk_b, memv_b)


def _sample_pre_kernel(x_ref, g_ref, win_ref, wpool_ref, ps_ref, wdw_ref, bdw_ref, lng_ref,
                       lnb_ref, wpw_ref, sp_ref, sc_ref,
                       part_ref, q_ref, gc_ref, pool_o_ref, conv_o_ref, y_ref):
    sb, t = SB_PRE, DEC_SEQ
    x = x_ref[...].reshape(t * sb, D)
    h = _rms(x, g_ref[...]).astype(BF)

    u3 = _dot(h, win_ref[:, 0:D]).reshape(t, sb, D)
    pool_o_ref[0:PB - t] = sp_ref[t:PB]
    pool_o_ref[PB - t:PB] = u3
    pos = PAST_LEN + lax.broadcasted_iota(jnp.int32, (t, 1, 1), 0)
    ys = []
    for gi, w in enumerate(WINDOWS):
        cols = slice(gi * GD, (gi + 1) * GD)
        s = u3[:, :, cols]
        for j in range(1, w):
            prev = sp_ref[PB - j:PB - j + min(j, t), :, cols]
            s = s + (prev if j >= t else jnp.concatenate([prev, u3[0:t - j, :, cols]], axis=0))
        inv_cnt = 1.0 / jnp.minimum(pos + 1, w).astype(F32)
        pooled = (s * inv_cnt - u3[:, :, cols]).reshape(t * sb, GD).astype(BF)
        ys.append(_dot(pooled, wpool_ref[gi]))
    y_pool = jnp.concatenate(ys, axis=-1) * ps_ref[...]
    part = jax.nn.sigmoid(_dot(h, win_ref[:, 4 * D:5 * D])) * y_pool

    z3 = (_dot(h, win_ref[:, D:2 * D]) *
          jax.nn.sigmoid(_dot(h, win_ref[:, 2 * D:3 * D]))).reshape(t, sb, D)
    conv_o_ref[0:CB - t] = sc_ref[t:CB]
    conv_o_ref[CB - t:CB] = z3
    for cb in range(NSLAB):
        for s0 in range(0, sb, SUBLANES):
            rows = slice(s0, s0 + SUBLANES)
            acc = None
            for k in range(CK):
                lo = sc_ref[k:t, rows, _slab(cb)] if k < t else None
                hi = conv_o_ref[max(k, t) - t:k, rows, _slab(cb)] if k > 0 else None
                win = lo if hi is None else hi if lo is None else jnp.concatenate([lo, hi], 0)
                term = win * wdw_ref[k:k + 1, _slab(cb)]
                acc = term if acc is None else acc + term
            y_ref[:, rows, _slab(cb)] = acc
    y_conv = _conv_ln_pw(y_ref[...].reshape(t * sb, D), bdw_ref, lng_ref, lnb_ref, wpw_ref)
    part = part + jax.nn.sigmoid(_dot(h, win_ref[:, 5 * D:6 * D])) * y_conv
    part_ref[...] = part.reshape(t, sb, D)

    q_ref[...] = _dot(h, win_ref[:, 3 * D:4 * D]).reshape(t, sb, D)
    gc_ref[...] = jax.nn.sigmoid(_dot(h, win_ref[:, 6 * D:7 * D])).reshape(t, sb, D)


def _sample_pre(xs, layer, mix_norm, w_in, w_pool, pool_scale, w_dw, b_dw, ln_g, ln_b, w_pw,
                state_pool_t, state_conv_t):
    vec = _const_spec((1, D), layer)
    tok = pl.BlockSpec((DEC_SEQ, SB_PRE, D), lambda i: (0, i, 0))
    tok_sds = jax.ShapeDtypeStruct((DEC_SEQ, DEC_BATCH, D), F32)
    return pl.pallas_call(
        _sample_pre_kernel,
        grid=(DEC_BATCH // SB_PRE,),
        in_specs=[tok, vec, _const_spec((D, 7 * D), layer),
                  _const_spec((len(WINDOWS), GD, GD), layer),
                  vec, _const_spec((CK, D), layer), vec, vec, vec, _const_spec((D, D), layer),
                  pl.BlockSpec((None, PB, SB_PRE, D), lambda i: (layer, 0, i, 0)),
                  pl.BlockSpec((None, CB, SB_PRE, D), lambda i: (layer, 0, i, 0))],
        out_specs=[tok, tok, tok,
                   pl.BlockSpec((PB, SB_PRE, D), lambda i: (0, i, 0)),
                   pl.BlockSpec((CB, SB_PRE, D), lambda i: (0, i, 0))],
        out_shape=[tok_sds, tok_sds, tok_sds,
                   jax.ShapeDtypeStruct((PB, DEC_BATCH, D), F32),
                   jax.ShapeDtypeStruct((CB, DEC_BATCH, D), F32)],
        scratch_shapes=[pltpu.VMEM((DEC_SEQ, SB_PRE, D), F32)],
        compiler_params=pltpu.CompilerParams(
            dimension_semantics=("arbitrary",), vmem_limit_bytes=VMEM_LIMIT),
        name="sample_pre",
    )(xs, mix_norm, w_in, w_pool, pool_scale, w_dw, b_dw, ln_g, ln_b, w_pw,
      state_pool_t, state_conv_t)


def _sample_att_kernel(x_ref, part_ref, q_ref, gc_ref, k_ref, v_ref, wout_ref, xo_ref,
                       qs_ref, ys_ref):
    sb, t = SB_ATT, DEC_SEQ
    half = NH * t
    for cb in range(NSLAB):
        qs_ref[cb] = q_ref[:, :, _slab(cb)].reshape(t * sb, LANES)

    row_head = lax.broadcasted_iota(jnp.int32, (half, KV_ROWS), 0) // t
    lane_slot = lax.broadcasted_iota(jnp.int32, (half, KV_ROWS), 1) % SUBLANES
    own = lane_slot == row_head

    for s in range(sb):
        qst = jnp.concatenate(
            [qs_ref[pl.ds(2 * hd + j, 1, stride=2), pl.ds(s, t, stride=sb), :].reshape(t, LANES)
             for j in range(2) for hd in range(NH)], axis=0).astype(BF)
        prod = _dot_t(qst, k_ref[s].astype(BF))
        sc = prod[0:half] + pltpu.roll(prod[half:2 * half], KV_ROWS - NH, axis=1)
        sc = jnp.where(own, sc * (DH ** -0.5), -jnp.inf)
        p = _softmax(sc)
        pst = jnp.concatenate([p, pltpu.roll(p, NH, axis=1)], axis=0).astype(BF)
        o = _dot(pst, v_ref[s].astype(BF))
        for j in range(2):
            for hd in range(NH):
                r0 = (j * NH + hd) * t
                ys_ref[pl.ds(2 * hd + j, 1, stride=2), pl.ds(s, t, stride=sb), :] = (
                    o[r0:r0 + t].reshape(1, t, LANES))

    y_att = jnp.concatenate([ys_ref[cb] for cb in range(NSLAB)], axis=-1)
    merged = part_ref[...].reshape(t * sb, D) + gc_ref[...].reshape(t * sb, D) * y_att
    out = x_ref[...].reshape(t * sb, D) + _dot(merged.astype(BF), wout_ref[...])
    xo_ref[...] = out.reshape(t, sb, D)


def _sample_att(xs, layer, part, q, gc, cache_k, cache_v, w_out):
    tok = pl.BlockSpec((DEC_SEQ, SB_ATT, D), lambda i: (0, i, 0))
    kv = pl.BlockSpec((None, SB_ATT, KV_ROWS, LANES), lambda i: (layer, i, 0, 0))
    return pl.pallas_call(
        _sample_att_kernel,
        grid=(DEC_BATCH // SB_ATT,),
        in_specs=[tok, tok, tok, tok, kv, kv, _const_spec((D, D), layer)],
        out_specs=tok,
        out_shape=jax.ShapeDtypeStruct((DEC_SEQ, DEC_BATCH, D), F32),
        scratch_shapes=[pltpu.VMEM((NSLAB, DEC_SEQ * SB_ATT, LANES), F32),
                        pltpu.VMEM((NSLAB, DEC_SEQ * SB_ATT, LANES), F32)],
        compiler_params=pltpu.CompilerParams(
            dimension_semantics=("arbitrary",), vmem_limit_bytes=VMEM_LIMIT),
        name="sample_att",
    )(xs, part, q, gc, cache_k, cache_v, w_out)


def _cache_rows(c):
    c = c.reshape(DEPTH, DEC_BATCH, NMEM, NH, DH // LANES, LANES)
    return c.transpose(0, 1, 2, 4, 3, 5).reshape(DEPTH, DEC_BATCH, KV_ROWS, LANES)


def kernel(x_prompt, x_sample, mem_prompt, state_pool, state_conv, cache_mem_k, cache_mem_v,
           ffn1_norm, ffn1_w_gu, ffn1_w_down, mix_norm, w_in, mem_norm, w_mem_kv, w_pool,
           pool_scale, w_dw, b_dw, conv_ln_g, conv_ln_b, w_pw, w_out, ffn2_norm, ffn2_w_gu,
           ffn2_w_down, final_norm):
    assert x_prompt.shape == (BATCH, SEQ, D) and x_sample.shape == (DEC_BATCH, DEC_SEQ, D)
    x = [x_prompt.reshape(NP, D), x_sample.transpose(1, 0, 2).reshape(NS, D)]

    def vec(a):
        return a.reshape(DEPTH, 1, D)

    ffn1_norm, mix_norm, mem_norm, pool_scale, b_dw, conv_ln_g, conv_ln_b, ffn2_norm = map(
        vec, (ffn1_norm, mix_norm, mem_norm, pool_scale, b_dw, conv_ln_g, conv_ln_b, ffn2_norm))
    ffn1_w_gu, ffn1_w_down, w_in, w_mem_kv, w_pool, w_pw, w_out, ffn2_w_gu, ffn2_w_down = (
        a.astype(BF) for a in (ffn1_w_gu, ffn1_w_down, w_in, w_mem_kv, w_pool, w_pw, w_out,
                               ffn2_w_gu, ffn2_w_down))
    cache_k = _cache_rows(cache_mem_k)
    cache_v = _cache_rows(cache_mem_v)
    state_pool_t = state_pool.transpose(0, 2, 1, 3)
    state_conv_t = state_conv.transpose(0, 2, 1, 3)

    memk, memv, memk_b, memv_b = _memory_kv(mem_prompt.reshape(BATCH * NMEM, D), mem_norm,
                                            w_mem_kv)

    pool_p, conv_p, pool_s, conv_s = [], [], [], []
    for l in range(DEPTH):
        x = _ffn(x, ffn1_norm, ffn1_w_gu, ffn1_w_down, l)
        mix_w = (mix_norm, w_in, w_pool, pool_scale, w_dw, b_dw, conv_ln_g, conv_ln_b, w_pw)
        xs = x[NP:].reshape(DEC_SEQ, DEC_BATCH, D)
        part, q, gc, spool, sconv = _sample_pre(xs, l, *mix_w, state_pool_t, state_conv_t)
        xs = _sample_att(xs, l, part, q, gc, cache_k, cache_v, w_out)
        x, ppool, pconv = _prompt_mix(x, l, *mix_w, w_out, memk_b, memv_b)
        x = lax.dynamic_update_slice(x, xs.reshape(NS, D), (NP, 0))
        pool_p.append(ppool)
        conv_p.append(pconv)
        pool_s.append(spool)
        conv_s.append(sconv)
        x = _ffn([x], ffn2_norm, ffn2_w_gu, ffn2_w_down, l,
                 final_norm=final_norm.reshape(1, D) if l == DEPTH - 1 else None)
        if l < DEPTH - 1:
            x = [x]

    y_prompt, y_sample = x
    y_sample = y_sample.reshape(DEC_SEQ, DEC_BATCH, D).transpose(1, 0, 2)
    return (y_prompt.reshape(BATCH, SEQ, D), y_sample,
            jnp.stack(pool_p), jnp.stack(conv_p), _cache_rows_inverse(memk),
            _cache_rows_inverse(memv),
            jnp.stack(pool_s).transpose(0, 2, 1, 3), jnp.stack(conv_s).transpose(0, 2, 1, 3))
```

```python
import functools

import jax
import jax.numpy as jnp
from jax import lax
from jax.experimental import pallas as pl
from jax.experimental.pallas import tpu as pltpu

F32 = jnp.float32
BF = jnp.bfloat16

D = 1024
DFF = 2816
DEPTH = 4
BATCH, SEQ = 8, 2048
DEC_BATCH, DEC_SEQ = 128, 8
PAST_LEN = 16384
WINDOWS = (2, 4, 8, 16)
GD = D // len(WINDOWS)
PB = max(WINDOWS) - 1
CK = 31
CB = CK - 1
N_GATES = 3
NMEM = 256
NH = 4
DH = D // NH
EPS = 1e-6
NP = BATCH * SEQ
NS = DEC_BATCH * DEC_SEQ
NT = NP + NS

SUBLANES = 8
LANES = 128
NSLAB = D // LANES
POOL_HALO = 16
CONV_HALO = 32
VMEM_LIMIT = 58 * 1024 * 1024

TM_FFN = 512
FFN_CHUNKS = (512, 512, 512, 512, 512, 256)
TM_MIX = 512
CONV_ROWS = 64
SB_PRE = 16
SB_ATT = 8
KV_ROWS = NMEM * NH * DH // LANES


def _dot(a, b):
    return jnp.dot(a, b, preferred_element_type=F32)


def _dot_t(a, b):
    return lax.dot_general(a, b, (((1,), (1,)), ((), ())), preferred_element_type=F32)


def _rms(x, g):
    return x * lax.rsqrt(jnp.mean(x * x, axis=-1, keepdims=True) + EPS) * g


def _silu(x):
    return x * jax.nn.sigmoid(x)


def _layernorm(y, g, b):
    mu = jnp.mean(y, axis=-1, keepdims=True)
    yc = y - mu
    var = jnp.mean(yc * yc, axis=-1, keepdims=True)
    return yc * lax.rsqrt(var + EPS) * g + b


def _softmax(s):
    e = jnp.exp(s - jnp.max(s, axis=-1, keepdims=True))
    return e / jnp.sum(e, axis=-1, keepdims=True)


def _conv_ln_pw(y, bdw_ref, lng_ref, lnb_ref, wpw_ref):
    yn = _layernorm(y + bdw_ref[...], lng_ref[...], lnb_ref[...])
    return _dot(_silu(yn).astype(BF), wpw_ref[...])


def _slab(cb):
    return slice(cb * LANES, (cb + 1) * LANES)


def _slab_rows(ref, cb, start, n):
    return ref[pl.ds(cb, 1, stride=2), pl.ds(start, n), :].reshape(n, LANES)


def _const_spec(shape, layer=None):
    if layer is None:
        return pl.BlockSpec(shape, lambda *_: (0,) * len(shape),
                            pipeline_mode=pl.Buffered(1))
    return pl.BlockSpec((None,) + shape, lambda *_: (layer,) + (0,) * len(shape),
                        pipeline_mode=pl.Buffered(1))


def _memkv_kernel(m_ref, g_ref, w_ref, k_ref, v_ref, kb_ref, vb_ref):
    h = _rms(m_ref[...], g_ref[...]).astype(BF)
    for c0, o_ref, ob_ref in ((0, k_ref, kb_ref), (D, v_ref, vb_ref)):
        kv = _dot(h, w_ref[:, c0:c0 + D])
        ob_ref[...] = kv.astype(BF)
        for j in range(DH // LANES):
            for hd in range(NH):
                o_ref[pl.ds(NH * j + hd, NMEM, stride=KV_ROWS // NMEM), :] = (
                    kv[:, hd * DH + j * LANES:hd * DH + (j + 1) * LANES])


def _memory_kv(mem, mem_norm, w_kv):
    raw = jax.ShapeDtypeStruct((DEPTH, BATCH, KV_ROWS, LANES), F32)
    mat = jax.ShapeDtypeStruct((DEPTH, BATCH * NMEM, D), BF)
    raw_spec = pl.BlockSpec((None, None, KV_ROWS, LANES), lambda l, b: (l, b, 0, 0))
    mat_spec = pl.BlockSpec((None, NMEM, D), lambda l, b: (l, b, 0))
    return pl.pallas_call(
        _memkv_kernel,
        grid=(DEPTH, BATCH),
        in_specs=[pl.BlockSpec((NMEM, D), lambda l, b: (b, 0)),
                  pl.BlockSpec((None, 1, D), lambda l, b: (l, 0, 0)),
                  pl.BlockSpec((None, D, 2 * D), lambda l, b: (l, 0, 0))],
        out_specs=[raw_spec, raw_spec, mat_spec, mat_spec],
        out_shape=[raw, raw, mat, mat],
        compiler_params=pltpu.CompilerParams(
            dimension_semantics=("arbitrary", "arbitrary"), vmem_limit_bytes=VMEM_LIMIT),
        name="memory_kv",
    )(mem, mem_norm, w_kv)


def _cache_rows_inverse(c):
    c = c.reshape(DEPTH, BATCH, NMEM, DH // LANES, NH, LANES)
    return c.transpose(0, 1, 2, 4, 3, 5).reshape(DEPTH, BATCH, NMEM, NH, DH)


NPB = NP // TM_FFN


def _ffn_kernel(*refs, first, final):
    refs = list(refs)
    a_ref = refs.pop()
    i = pl.program_id(0)
    if first:
        xp_ref, xs_ref = refs.pop(0), refs.pop(0)
        x = jnp.where(i < NPB, xp_ref[...], xs_ref[...])
    else:
        x = refs.pop(0)[...]
    g_ref, wgu_ref, wd_ref = refs[:3]
    h = _rms(x, g_ref[...]).astype(BF)
    c0 = 0
    for width in FFN_CHUNKS:
        gate = _dot(h, wgu_ref[:, c0:c0 + width].astype(BF))
        up = _dot(h, wgu_ref[:, DFF + c0:DFF + c0 + width].astype(BF))
        a_ref[:, c0:c0 + width] = (_silu(gate) * up).astype(BF)
        c0 += width
    y = x + 0.5 * _dot(a_ref[...], wd_ref[...].astype(BF))
    if not final:
        refs[3][...] = y
        return
    fg_ref, yp_ref, ys_ref = refs[3:]
    y = _rms(y, fg_ref[...])

    @pl.when(i < NPB)
    def _():
        yp_ref[...] = y

    @pl.when(i >= NPB)
    def _():
        ys_ref[...] = y


def _ffn(xs, norm, w_gu, w_down, layer, final_norm=None):
    first, final = len(xs) == 2, final_norm is not None
    tok_spec = pl.BlockSpec((TM_FFN, D), lambda i: (i, 0))
    prompt_spec = pl.BlockSpec((TM_FFN, D), lambda i: (jnp.minimum(i, NPB - 1), 0))
    sample_spec = pl.BlockSpec((TM_FFN, D), lambda i: (jnp.maximum(i - NPB, 0), 0))
    in_specs = ([prompt_spec, sample_spec] if first else [tok_spec]) + [
        _const_spec((1, D), layer), _const_spec((D, 2 * DFF), layer),
        _const_spec((DFF, D), layer)]
    args = list(xs) + [norm, w_gu, w_down]
    if final:
        in_specs.append(_const_spec((1, D)))
        args.append(final_norm)
        out_specs = [prompt_spec, sample_spec]
        out_shape = [jax.ShapeDtypeStruct((NP, D), F32), jax.ShapeDtypeStruct((NS, D), F32)]
    else:
        out_specs = tok_spec
        out_shape = jax.ShapeDtypeStruct((NT, D), F32)
    return pl.pallas_call(
        functools.partial(_ffn_kernel, first=first, final=final),
        grid=(NT // TM_FFN,),
        in_specs=in_specs,
        out_specs=out_specs,
        out_shape=out_shape,
        scratch_shapes=[pltpu.VMEM((TM_FFN, DFF), BF)],
        compiler_params=pltpu.CompilerParams(
            dimension_semantics=("arbitrary",), vmem_limit_bytes=VMEM_LIMIT),
        name="ffn_first" if first else "ffn_final" if final else "ffn",
    )(*args)


def _prompt_mix_kernel(x_ref, g_ref, win_ref, wpool_ref, ps_ref, wdw_ref, bdw_ref, lng_ref,
                       lnb_ref, wpw_ref, wout_ref, k_ref, v_ref,
                       xo_ref, pool_o_ref, conv_o_ref,
                       extu_ref, extz_ref, y_ref, h_ref, q_ref, gate_ref):
    tm = TM_MIX
    c = pl.program_id(1)
    ngroups = len(WINDOWS)

    @pl.when(c == 0)
    def _():
        extu_ref[:, 0:POOL_HALO, :] = jnp.zeros((NSLAB, POOL_HALO, LANES), F32)
        extz_ref[:, 0:CONV_HALO, :] = jnp.zeros((NSLAB, CONV_HALO, LANES), F32)

    x = x_ref[...]
    h_ref[...] = _rms(x, g_ref[...]).astype(BF)

    slabs = GD // LANES
    first = CONV_HALO - CB

    def proj(kind, gi):
        col = kind * D + gi * GD
        return _dot(h_ref[...], win_ref[:, col:col + GD])

    for gi in range(ngroups):
        z = proj(1, gi) * jax.nn.sigmoid(proj(2, gi))
        for s in range(slabs):
            extz_ref[gi * slabs + s, CONV_HALO:CONV_HALO + tm, :] = z[:, _slab(s)]
        for s in range(slabs):
            cb = gi * slabs + s
            for r0 in range(0, tm, CONV_ROWS):
                acc = None
                for k in range(CK):
                    term = (_slab_rows(extz_ref, cb, first + r0 + k, CONV_ROWS) *
                            wdw_ref[k:k + 1, _slab(cb)])
                    acc = term if acc is None else acc + term
                y_ref[cb, r0:r0 + CONV_ROWS, :] = acc
        u = proj(0, gi)
        for s in range(slabs):
            extu_ref[gi * slabs + s, POOL_HALO:POOL_HALO + tm, :] = u[:, _slab(s)]
        q_ref[gi] = proj(3, gi).astype(BF)
        for n in range(N_GATES):
            gate_ref[n, gi] = jax.nn.sigmoid(proj(4 + n, gi))
    for cb in range(NSLAB):
        tail_z = _slab_rows(extz_ref, cb, tm + CONV_HALO - CB, CB)
        conv_o_ref[:, _slab(cb)] = tail_z
        extz_ref[cb, CONV_HALO - CB:CONV_HALO, :] = tail_z

    group_cols = [slice(gi * GD, (gi + 1) * GD) for gi in range(ngroups)]

    scores = [_dot_t(q_ref[gi], k_ref[:, group_cols[gi]]) * (DH ** -0.5)
              for gi in range(ngroups)]

    y_conv = _conv_ln_pw(jnp.concatenate([y_ref[cb] for cb in range(NSLAB)], axis=-1),
                         bdw_ref, lng_ref, lnb_ref, wpw_ref)

    pos = c * tm + lax.broadcasted_iota(jnp.int32, (tm, 1), 0)
    pooled = []
    for gi, w in enumerate(WINDOWS):
        inv_cnt = 1.0 / jnp.minimum(pos + 1, w).astype(F32)
        parts = []
        for cb in range(gi * GD // LANES, (gi + 1) * GD // LANES):
            u = extu_ref[cb, POOL_HALO:POOL_HALO + tm, :]
            s = u
            for j in range(1, w):
                s = s + _slab_rows(extu_ref, cb, POOL_HALO - j, tm)
            parts.append(s * inv_cnt - u)
        pooled.append(jnp.concatenate(parts, axis=-1).astype(BF))
    probs = [_softmax(s).astype(BF) for s in scores]

    y_pool = [_dot(pooled[gi], wpool_ref[gi]) * ps_ref[:, group_cols[gi]]
              for gi in range(ngroups)]
    y_att = [_dot(probs[gi], v_ref[:, group_cols[gi]]) for gi in range(ngroups)]
    merged = [gate_ref[0, gi] * y_pool[gi] + gate_ref[1, gi] * y_conv[:, group_cols[gi]] +
              gate_ref[2, gi] * y_att[gi] for gi in range(ngroups)]
    for cb in range(NSLAB):
        tail_u = _slab_rows(extu_ref, cb, tm + POOL_HALO - PB, PB)
        pool_o_ref[:, _slab(cb)] = tail_u
        extu_ref[cb, POOL_HALO - PB:POOL_HALO, :] = tail_u

    xo_ref[...] = x + _dot(jnp.concatenate(merged, axis=-1).astype(BF), wout_ref[...])


def _prompt_mix(x, layer, mix_norm, w_in, w_pool, pool_scale, w_dw, b_dw, ln_g, ln_b, w_pw,
                w_out, memk_b, memv_b):
    nc = SEQ // TM_MIX
    tok_spec = pl.BlockSpec((TM_MIX, D), lambda b, c: (b * nc + c, 0))
    mem_spec = pl.BlockSpec((None, NMEM, D), lambda b, c: (layer, b, 0))
    vec = _const_spec((1, D), layer)
    return pl.pallas_call(
        _prompt_mix_kernel,
        grid=(BATCH, nc),
        in_specs=[tok_spec, vec, _const_spec((D, 7 * D), layer),
                  _const_spec((len(WINDOWS), GD, GD), layer), vec, _const_spec((CK, D), layer),
                  vec, vec, vec, _const_spec((D, D), layer), _const_spec((D, D), layer),
                  mem_spec, mem_spec],
        out_specs=[tok_spec,
                   pl.BlockSpec((None, PB, D), lambda b, c: (b, 0, 0)),
                   pl.BlockSpec((None, CB, D), lambda b, c: (b, 0, 0))],
        out_shape=[jax.ShapeDtypeStruct((NT, D), F32),
                   jax.ShapeDtypeStruct((BATCH, PB, D), F32),
                   jax.ShapeDtypeStruct((BATCH, CB, D), F32)],
        scratch_shapes=[pltpu.VMEM((NSLAB, POOL_HALO + TM_MIX, LANES), F32),
                        pltpu.VMEM((NSLAB, CONV_HALO + TM_MIX, LANES), F32),
                        pltpu.VMEM((NSLAB, TM_MIX, LANES), F32),
                        pltpu.VMEM((TM_MIX, D), BF),
                        pltpu.VMEM((len(WINDOWS), TM_MIX, GD), BF),
                        pltpu.VMEM((N_GATES, len(WINDOWS), TM_MIX, GD), F32)],
        input_output_aliases={0: 0},
        compiler_params=pltpu.CompilerParams(
            dimension_semantics=("arbitrary", "arbitrary"), vmem_limit_bytes=VMEM_LIMIT),
        name="prompt_mix",
    )(x, mix_norm, w_in, w_pool, pool_scale, w_dw, b_dw, ln_g, ln_b, w_pw, w_out,
      memk_b, memv_b)


def _sample_pre_kernel(x_ref, g_ref, win_ref, wpool_ref, ps_ref, wdw_ref, bdw_ref, lng_ref,
                       lnb_ref, wpw_ref, sp_ref, sc_ref,
                       part_ref, q_ref, gc_ref, pool_o_ref, conv_o_ref, y_ref):
    sb, t = SB_PRE, DEC_SEQ
    x = x_ref[...].reshape(t * sb, D)
    h = _rms(x, g_ref[...]).astype(BF)

    u3 = _dot(h, win_ref[:, 0:D]).reshape(t, sb, D)
    pool_o_ref[0:PB - t] = sp_ref[t:PB]
    pool_o_ref[PB - t:PB] = u3
    pos = PAST_LEN + lax.broadcasted_iota(jnp.int32, (t, 1, 1), 0)
    ys = []
    for gi, w in enumerate(WINDOWS):
        cols = slice(gi * GD, (gi + 1) * GD)
        s = u3[:, :, cols]
        for j in range(1, w):
            prev = sp_ref[PB - j:PB - j + min(j, t), :, cols]
            s = s + (prev if j >= t else jnp.concatenate([prev, u3[0:t - j, :, cols]], axis=0))
        inv_cnt = 1.0 / jnp.minimum(pos + 1, w).astype(F32)
        pooled = (s * inv_cnt - u3[:, :, cols]).reshape(t * sb, GD).astype(BF)
        ys.append(_dot(pooled, wpool_ref[gi]))
    y_pool = jnp.concatenate(ys, axis=-1) * ps_ref[...]
    part = jax.nn.sigmoid(_dot(h, win_ref[:, 4 * D:5 * D])) * y_pool

    z3 = (_dot(h, win_ref[:, D:2 * D]) *
          jax.nn.sigmoid(_dot(h, win_ref[:, 2 * D:3 * D]))).reshape(t, sb, D)
    conv_o_ref[0:CB - t] = sc_ref[t:CB]
    conv_o_ref[CB - t:CB] = z3
    for cb in range(NSLAB):
        for s0 in range(0, sb, SUBLANES):
            rows = slice(s0, s0 + SUBLANES)
            acc = None
            for k in range(CK):
                lo = sc_ref[k:t, rows, _slab(cb)] if k < t else None
                hi = conv_o_ref[max(k, t) - t:k, rows, _slab(cb)] if k > 0 else None
                win = lo if hi is None else hi if lo is None else jnp.concatenate([lo, hi], 0)
                term = win * wdw_ref[k:k + 1, _slab(cb)]
                acc = term if acc is None else acc + term
            y_ref[:, rows, _slab(cb)] = acc
    y_conv = _conv_ln_pw(y_ref[...].reshape(t * sb, D), bdw_ref, lng_ref, lnb_ref, wpw_ref)
    part = part + jax.nn.sigmoid(_dot(h, win_ref[:, 5 * D:6 * D])) * y_conv
    part_ref[...] = part.reshape(t, sb, D)

    q_ref[...] = _dot(h, win_ref[:, 3 * D:4 * D]).reshape(t, sb, D)
    gc_ref[...] = jax.nn.sigmoid(_dot(h, win_ref[:, 6 * D:7 * D])).reshape(t, sb, D)


def _sample_pre(xs, layer, mix_norm, w_in, w_pool, pool_scale, w_dw, b_dw, ln_g, ln_b, w_pw,
                state_pool_t, state_conv_t):
    vec = _const_spec((1, D), layer)
    tok = pl.BlockSpec((DEC_SEQ, SB_PRE, D), lambda i: (0, i, 0))
    tok_sds = jax.ShapeDtypeStruct((DEC_SEQ, DEC_BATCH, D), F32)
    return pl.pallas_call(
        _sample_pre_kernel,
        grid=(DEC_BATCH // SB_PRE,),
        in_specs=[tok, vec, _const_spec((D, 7 * D), layer),
                  _const_spec((len(WINDOWS), GD, GD), layer),
                  vec, _const_spec((CK, D), layer), vec, vec, vec, _const_spec((D, D), layer),
                  pl.BlockSpec((None, PB, SB_PRE, D), lambda i: (layer, 0, i, 0)),
                  pl.BlockSpec((None, CB, SB_PRE, D), lambda i: (layer, 0, i, 0))],
        out_specs=[tok, tok, tok,
                   pl.BlockSpec((PB, SB_PRE, D), lambda i: (0, i, 0)),
                   pl.BlockSpec((CB, SB_PRE, D), lambda i: (0, i, 0))],
        out_shape=[tok_sds, tok_sds, tok_sds,
                   jax.ShapeDtypeStruct((PB, DEC_BATCH, D), F32),
                   jax.ShapeDtypeStruct((CB, DEC_BATCH, D), F32)],
        scratch_shapes=[pltpu.VMEM((DEC_SEQ, SB_PRE, D), F32)],
        compiler_params=pltpu.CompilerParams(
            dimension_semantics=("arbitrary",), vmem_limit_bytes=VMEM_LIMIT),
        name="sample_pre",
    )(xs, mix_norm, w_in, w_pool, pool_scale, w_dw, b_dw, ln_g, ln_b, w_pw,
      state_pool_t, state_conv_t)


def _sample_att_kernel(x_ref, part_ref, q_ref, gc_ref, k_ref, v_ref, wout_ref, xo_ref,
                       qs_ref, ys_ref):
    sb, t = SB_ATT, DEC_SEQ
    half = NH * t
    for cb in range(NSLAB):
        qs_ref[cb] = q_ref[:, :, _slab(cb)].reshape(t * sb, LANES)

    row_head = lax.broadcasted_iota(jnp.int32, (half, KV_ROWS), 0) // t
    lane_slot = lax.broadcasted_iota(jnp.int32, (half, KV_ROWS), 1) % SUBLANES
    own = lane_slot == row_head

    prods = []
    for s in range(sb):
        qst = jnp.concatenate(
            [qs_ref[pl.ds(2 * hd + j, 1, stride=2), pl.ds(s, t, stride=sb), :].reshape(t, LANES)
             for j in range(2) for hd in range(NH)], axis=0).astype(BF)
        prods.append(_dot_t(qst, k_ref[s].astype(BF)))
    for s in range(sb):
        prod = prods[s]
        sc = prod[0:half] + pltpu.roll(prod[half:2 * half], KV_ROWS - NH, axis=1)
        sc = jnp.where(own, sc * (DH ** -0.5), -jnp.inf)
        p = _softmax(sc)
        pst = jnp.concatenate([p, pltpu.roll(p, NH, axis=1)], axis=0).astype(BF)
        o = _dot(pst, v_ref[s].astype(BF))
        for j in range(2):
            for hd in range(NH):
                r0 = (j * NH + hd) * t
                ys_ref[pl.ds(2 * hd + j, 1, stride=2), pl.ds(s, t, stride=sb), :] = (
                    o[r0:r0 + t].reshape(1, t, LANES))

    y_att = jnp.concatenate([ys_ref[cb] for cb in range(NSLAB)], axis=-1)
    merged = part_ref[...].reshape(t * sb, D) + gc_ref[...].reshape(t * sb, D) * y_att
    out = x_ref[...].reshape(t * sb, D) + _dot(merged.astype(BF), wout_ref[...])
    xo_ref[...] = out.reshape(t, sb, D)


def _sample_att(xs, layer, part, q, gc, cache_k, cache_v, w_out):
    tok = pl.BlockSpec((DEC_SEQ, SB_ATT, D), lambda i: (0, i, 0))
    kv = pl.BlockSpec((None, SB_ATT, KV_ROWS, LANES), lambda i: (layer, i, 0, 0))
    return pl.pallas_call(
        _sample_att_kernel,
        grid=(DEC_BATCH // SB_ATT,),
        in_specs=[tok, tok, tok, tok, kv, kv, _const_spec((D, D), layer)],
        out_specs=tok,
        out_shape=jax.ShapeDtypeStruct((DEC_SEQ, DEC_BATCH, D), F32),
        scratch_shapes=[pltpu.VMEM((NSLAB, DEC_SEQ * SB_ATT, LANES), F32),
                        pltpu.VMEM((NSLAB, DEC_SEQ * SB_ATT, LANES), F32)],
        compiler_params=pltpu.CompilerParams(
            dimension_semantics=("arbitrary",), vmem_limit_bytes=VMEM_LIMIT),
        name="sample_att",
    )(xs, part, q, gc, cache_k, cache_v, w_out)


def _cache_rows(c):
    c = c.reshape(DEPTH, DEC_BATCH, NMEM, NH, DH // LANES, LANES)
    return c.transpose(0, 1, 2, 4, 3, 5).reshape(DEPTH, DEC_BATCH, KV_ROWS, LANES)


def kernel(x_prompt, x_sample, mem_prompt, state_pool, state_conv, cache_mem_k, cache_mem_v,
           ffn1_norm, ffn1_w_gu, ffn1_w_down, mix_norm, w_in, mem_norm, w_mem_kv, w_pool,
           pool_scale, w_dw, b_dw, conv_ln_g, conv_ln_b, w_pw, w_out, ffn2_norm, ffn2_w_gu,
           ffn2_w_down, final_norm):
    assert x_prompt.shape == (BATCH, SEQ, D) and x_sample.shape == (DEC_BATCH, DEC_SEQ, D)
    x = [x_prompt.reshape(NP, D), x_sample.transpose(1, 0, 2).reshape(NS, D)]

    def vec(a):
        return a.reshape(DEPTH, 1, D)

    ffn1_norm, mix_norm, mem_norm, pool_scale, b_dw, conv_ln_g, conv_ln_b, ffn2_norm = map(
        vec, (ffn1_norm, mix_norm, mem_norm, pool_scale, b_dw, conv_ln_g, conv_ln_b, ffn2_norm))
    w_in, w_mem_kv, w_pool, w_pw, w_out = (
        a.astype(BF) for a in (w_in, w_mem_kv, w_pool, w_pw, w_out))
    cache_k = _cache_rows(cache_mem_k)
    cache_v = _cache_rows(cache_mem_v)
    state_pool_t = state_pool.transpose(0, 2, 1, 3)
    state_conv_t = state_conv.transpose(0, 2, 1, 3)

    memk, memv, memk_b, memv_b = _memory_kv(mem_prompt.reshape(BATCH * NMEM, D), mem_norm,
                                            w_mem_kv)

    pool_p, conv_p, pool_s, conv_s = [], [], [], []
    for l in range(DEPTH):
        x = _ffn(x, ffn1_norm, ffn1_w_gu, ffn1_w_down, l)
        mix_w = (mix_norm, w_in, w_pool, pool_scale, w_dw, b_dw, conv_ln_g, conv_ln_b, w_pw)
        xs = x[NP:].reshape(DEC_SEQ, DEC_BATCH, D)
        part, q, gc, spool, sconv = _sample_pre(xs, l, *mix_w, state_pool_t, state_conv_t)
        xs = _sample_att(xs, l, part, q, gc, cache_k, cache_v, w_out)
        x, ppool, pconv = _prompt_mix(x, l, *mix_w, w_out, memk_b, memv_b)
        x = lax.dynamic_update_slice(x, xs.reshape(NS, D), (NP, 0))
        pool_p.append(ppool)
        conv_p.append(pconv)
        pool_s.append(spool)
        conv_s.append(sconv)
        x = _ffn([x], ffn2_norm, ffn2_w_gu, ffn2_w_down, l,
                 final_norm=final_norm.reshape(1, D) if l == DEPTH - 1 else None)
        if l < DEPTH - 1:
            x = [x]

    y_prompt, y_sample = x
    y_sample = y_sample.reshape(DEC_SEQ, DEC_BATCH, D).transpose(1, 0, 2)
    return (y_prompt.reshape(BATCH, SEQ, D), y_sample,
            jnp.stack(pool_p), jnp.stack(conv_p), _cache_rows_inverse(memk),
            _cache_rows_inverse(memv),
            jnp.stack(pool_s).transpose(0, 2, 1, 3), jnp.stack(conv_s).transpose(0, 2, 1, 3))
```

```python
import functools

import jax
import jax.numpy as jnp
from jax import lax
from jax.experimental import pallas as pl
from jax.experimental.pallas import tpu as pltpu

F32 = jnp.float32
BF = jnp.bfloat16

D = 1024
DFF = 2816
DEPTH = 4
BATCH, SEQ = 8, 2048
DEC_BATCH, DEC_SEQ = 128, 8
PAST_LEN = 16384
WINDOWS = (2, 4, 8, 16)
GD = D // len(WINDOWS)
PB = max(WINDOWS) - 1
CK = 31
CB = CK - 1
N_GATES = 3
NMEM = 256
NH = 4
DH = D // NH
EPS = 1e-6
NP = BATCH * SEQ
NS = DEC_BATCH * DEC_SEQ
NT = NP + NS
SAMPLE_PLANE_BLOCK = NP // (DEC_SEQ * DEC_BATCH)

SUBLANES = 8
LANES = 128
NSLAB = D // LANES
POOL_HALO = 16
CONV_HALO = 32
VMEM_LIMIT = 58 * 1024 * 1024

TM_FFN = 512
FFN_CHUNKS = (512, 512, 512, 512, 512, 256)
TM_MIX = 512
CONV_ROWS = 32
SB_PRE = 16
SB_ATT = 8
KV_BATCH = 2
KV_ROWS = NMEM * NH * DH // LANES


def _dot(a, b):
    return jnp.dot(a, b, preferred_element_type=F32)


def _dot_t(a, b):
    return lax.dot_general(a, b, (((1,), (1,)), ((), ())), preferred_element_type=F32)


def _rms(x, g):
    return x * lax.rsqrt(jnp.mean(x * x, axis=-1, keepdims=True) + EPS) * g


def _silu(x):
    return x * jax.nn.sigmoid(x)


def _layernorm(y, g, b):
    mu = jnp.mean(y, axis=-1, keepdims=True)
    yc = y - mu
    var = jnp.mean(yc * yc, axis=-1, keepdims=True)
    return yc * lax.rsqrt(var + EPS) * g + b


def _softmax(s):
    e = jnp.exp(s - jnp.max(s, axis=-1, keepdims=True))
    return e / jnp.sum(e, axis=-1, keepdims=True)


def _conv_ln_pw(y, bdw_ref, lng_ref, lnb_ref, wpw_ref):
    yn = _layernorm(y + bdw_ref[...], lng_ref[...], lnb_ref[...])
    return _dot(_silu(yn).astype(BF), wpw_ref[...])


def _slab(cb):
    return slice(cb * LANES, (cb + 1) * LANES)


def _slab_rows(ref, cb, start, n):
    return ref[pl.ds(cb, 1, stride=2), pl.ds(start, n), :].reshape(n, LANES)


def _const_spec(shape, layer=None):
    if layer is None:
        return pl.BlockSpec(shape, lambda *_: (0,) * len(shape),
                            pipeline_mode=pl.Buffered(1))
    return pl.BlockSpec((None,) + shape, lambda *_: (layer,) + (0,) * len(shape),
                        pipeline_mode=pl.Buffered(1))


def _memkv_kernel(m_ref, g_ref, w_ref, k_ref, v_ref, kb_ref, vb_ref):
    h = _rms(m_ref[...], g_ref[...]).astype(BF)
    for c0, o_ref, ob_ref in ((0, k_ref, kb_ref), (D, v_ref, vb_ref)):
        kv = _dot(h, w_ref[:, c0:c0 + D])
        ob_ref[...] = kv.astype(BF)
        for e in range(KV_BATCH):
            for j in range(DH // LANES):
                for hd in range(NH):
                    o_ref[e, pl.ds(NH * j + hd, NMEM, stride=KV_ROWS // NMEM), :] = (
                        kv[e * NMEM:(e + 1) * NMEM, hd * DH + j * LANES:hd * DH + (j + 1) * LANES])


def _memory_kv(mem, mem_norm, w_kv):
    raw = jax.ShapeDtypeStruct((DEPTH, BATCH, KV_ROWS, LANES), F32)
    mat = jax.ShapeDtypeStruct((DEPTH, BATCH * NMEM, D), BF)
    raw_spec = pl.BlockSpec((None, KV_BATCH, KV_ROWS, LANES), lambda l, b: (l, b, 0, 0))
    mat_spec = pl.BlockSpec((None, KV_BATCH * NMEM, D), lambda l, b: (l, b, 0))
    return pl.pallas_call(
        _memkv_kernel,
        grid=(DEPTH, BATCH // KV_BATCH),
        in_specs=[pl.BlockSpec((KV_BATCH * NMEM, D), lambda l, b: (b, 0)),
                  pl.BlockSpec((None, 1, D), lambda l, b: (l, 0, 0)),
                  pl.BlockSpec((None, D, 2 * D), lambda l, b: (l, 0, 0))],
        out_specs=[raw_spec, raw_spec, mat_spec, mat_spec],
        out_shape=[raw, raw, mat, mat],
        compiler_params=pltpu.CompilerParams(
            dimension_semantics=("arbitrary", "arbitrary"), vmem_limit_bytes=VMEM_LIMIT),
        name="memory_kv",
    )(mem, mem_norm, w_kv)


def _cache_rows_inverse(c):
    c = c.reshape(DEPTH, BATCH, NMEM, DH // LANES, NH, LANES)
    return c.transpose(0, 1, 2, 4, 3, 5).reshape(DEPTH, BATCH, NMEM, NH, DH)


NPB = NP // TM_FFN


def _ffn_kernel(*refs, first, final):
    refs = list(refs)
    a_ref = refs.pop()
    i = pl.program_id(0)
    if first:
        xp_ref, xs_ref = refs.pop(0), refs.pop(0)
        x = jnp.where(i < NPB, xp_ref[...], xs_ref[...])
    else:
        x = refs.pop(0)[...]
    g_ref, wgu_ref, wd_ref = refs[:3]
    h = _rms(x, g_ref[...]).astype(BF)
    c0 = 0
    for width in FFN_CHUNKS:
        gate = _dot(h, wgu_ref[:, c0:c0 + width].astype(BF))
        up = _dot(h, wgu_ref[:, DFF + c0:DFF + c0 + width].astype(BF))
        a_ref[:, c0:c0 + width] = (_silu(gate) * up).astype(BF)
        c0 += width
    y = x + 0.5 * _dot(a_ref[...], wd_ref[...].astype(BF))
    if not final:
        refs[3][...] = y
        return
    fg_ref, yp_ref, ys_ref = refs[3:]
    y = _rms(y, fg_ref[...])

    @pl.when(i < NPB)
    def _():
        yp_ref[...] = y

    @pl.when(i >= NPB)
    def _():
        ys_ref[...] = y


def _ffn(xs, norm, w_gu, w_down, layer, final_norm=None):
    first, final = len(xs) == 2, final_norm is not None
    tok_spec = pl.BlockSpec((TM_FFN, D), lambda i: (i, 0))
    prompt_spec = pl.BlockSpec((TM_FFN, D), lambda i: (jnp.minimum(i, NPB - 1), 0))
    sample_spec = pl.BlockSpec((TM_FFN, D), lambda i: (jnp.maximum(i - NPB, 0), 0))
    in_specs = ([prompt_spec, sample_spec] if first else [tok_spec]) + [
        _const_spec((1, D), layer), _const_spec((D, 2 * DFF), layer),
        _const_spec((DFF, D), layer)]
    args = list(xs) + [norm, w_gu, w_down]
    if final:
        in_specs.append(_const_spec((1, D)))
        args.append(final_norm)
        out_specs = [prompt_spec, sample_spec]
        out_shape = [jax.ShapeDtypeStruct((NP, D), F32), jax.ShapeDtypeStruct((NS, D), F32)]
    else:
        out_specs = tok_spec
        out_shape = jax.ShapeDtypeStruct((NT, D), F32)
    return pl.pallas_call(
        functools.partial(_ffn_kernel, first=first, final=final),
        grid=(NT // TM_FFN,),
        in_specs=in_specs,
        out_specs=out_specs,
        out_shape=out_shape,
        scratch_shapes=[pltpu.VMEM((TM_FFN, DFF), BF)],
        compiler_params=pltpu.CompilerParams(
            dimension_semantics=("arbitrary",), vmem_limit_bytes=VMEM_LIMIT),
        name="ffn_first" if first else "ffn_final" if final else "ffn",
    )(*args)


def _prompt_mix_kernel(x_ref, g_ref, win_ref, wpool_ref, ps_ref, wdw_ref, bdw_ref, lng_ref,
                       lnb_ref, wpw_ref, wout_ref, k_ref, v_ref,
                       xo_ref, pool_o_ref, conv_o_ref,
                       extu_ref, extz_ref, y_ref, h_ref, q_ref, gate_ref):
    tm = TM_MIX
    c = pl.program_id(1)
    ngroups = len(WINDOWS)

    @pl.when(c == 0)
    def _():
        extu_ref[:, 0:POOL_HALO, :] = jnp.zeros((NSLAB, POOL_HALO, LANES), F32)
        extz_ref[:, 0:CONV_HALO, :] = jnp.zeros((NSLAB, CONV_HALO, LANES), F32)

    x = x_ref[...]
    h_ref[...] = _rms(x, g_ref[...]).astype(BF)

    slabs = GD // LANES
    first = CONV_HALO - CB

    def proj(kind, gi):
        col = kind * D + gi * GD
        return _dot(h_ref[...], win_ref[:, col:col + GD])

    for gi in range(ngroups):
        z = proj(1, gi) * jax.nn.sigmoid(proj(2, gi))
        for s in range(slabs):
            extz_ref[gi * slabs + s, CONV_HALO:CONV_HALO + tm, :] = z[:, _slab(s)]
        for s in range(slabs):
            cb = gi * slabs + s
            for r0 in range(0, tm, CONV_ROWS):
                acc = None
                for k in range(CK):
                    term = (_slab_rows(extz_ref, cb, first + r0 + k, CONV_ROWS) *
                            wdw_ref[k:k + 1, _slab(cb)])
                    acc = term if acc is None else acc + term
                y_ref[cb, r0:r0 + CONV_ROWS, :] = acc
        u = proj(0, gi)
        for s in range(slabs):
            extu_ref[gi * slabs + s, POOL_HALO:POOL_HALO + tm, :] = u[:, _slab(s)]
        q_ref[gi] = proj(3, gi).astype(BF)
        for n in range(N_GATES):
            gate_ref[n, gi] = jax.nn.sigmoid(proj(4 + n, gi))
    for cb in range(NSLAB):
        tail_z = _slab_rows(extz_ref, cb, tm + CONV_HALO - CB, CB)
        conv_o_ref[:, _slab(cb)] = tail_z
        extz_ref[cb, CONV_HALO - CB:CONV_HALO, :] = tail_z

    group_cols = [slice(gi * GD, (gi + 1) * GD) for gi in range(ngroups)]

    scores = [_dot_t(q_ref[gi], k_ref[:, group_cols[gi]]) * (DH ** -0.5)
              for gi in range(ngroups)]

    y_conv = _conv_ln_pw(jnp.concatenate([y_ref[cb] for cb in range(NSLAB)], axis=-1),
                         bdw_ref, lng_ref, lnb_ref, wpw_ref)

    pos = c * tm + lax.broadcasted_iota(jnp.int32, (tm, 1), 0)
    pooled = []
    for gi, w in enumerate(WINDOWS):
        inv_cnt = 1.0 / jnp.minimum(pos + 1, w).astype(F32)
        parts = []
        for cb in range(gi * GD // LANES, (gi + 1) * GD // LANES):
            u = extu_ref[cb, POOL_HALO:POOL_HALO + tm, :]
            s = u
            for j in range(1, w):
                s = s + _slab_rows(extu_ref, cb, POOL_HALO - j, tm)
            parts.append(s * inv_cnt - u)
        pooled.append(jnp.concatenate(parts, axis=-1).astype(BF))
    probs = [_softmax(s).astype(BF) for s in scores]

    y_pool = [_dot(pooled[gi], wpool_ref[gi]) * ps_ref[:, group_cols[gi]]
              for gi in range(ngroups)]
    y_att = [_dot(probs[gi], v_ref[:, group_cols[gi]]) for gi in range(ngroups)]
    merged = [gate_ref[0, gi] * y_pool[gi] + gate_ref[1, gi] * y_conv[:, group_cols[gi]] +
              gate_ref[2, gi] * y_att[gi] for gi in range(ngroups)]
    for cb in range(NSLAB):
        tail_u = _slab_rows(extu_ref, cb, tm + POOL_HALO - PB, PB)
        pool_o_ref[:, _slab(cb)] = tail_u
        extu_ref[cb, POOL_HALO - PB:POOL_HALO, :] = tail_u

    xo_ref[...] = x + _dot(jnp.concatenate(merged, axis=-1).astype(BF), wout_ref[...])


def _prompt_mix(x, layer, mix_norm, w_in, w_pool, pool_scale, w_dw, b_dw, ln_g, ln_b, w_pw,
                w_out, memk_b, memv_b):
    nc = SEQ // TM_MIX
    tok_spec = pl.BlockSpec((TM_MIX, D), lambda b, c: (b * nc + c, 0))
    mem_spec = pl.BlockSpec((None, NMEM, D), lambda b, c: (layer, b, 0))
    vec = _const_spec((1, D), layer)
    return pl.pallas_call(
        _prompt_mix_kernel,
        grid=(BATCH, nc),
        in_specs=[tok_spec, vec, _const_spec((D, 7 * D), layer),
                  _const_spec((len(WINDOWS), GD, GD), layer), vec, _const_spec((CK, D), layer),
                  vec, vec, vec, _const_spec((D, D), layer), _const_spec((D, D), layer),
                  mem_spec, mem_spec],
        out_specs=[tok_spec,
                   pl.BlockSpec((None, PB, D), lambda b, c: (b, 0, 0)),
                   pl.BlockSpec((None, CB, D), lambda b, c: (b, 0, 0))],
        out_shape=[jax.ShapeDtypeStruct((NT, D), F32),
                   jax.ShapeDtypeStruct((BATCH, PB, D), F32),
                   jax.ShapeDtypeStruct((BATCH, CB, D), F32)],
        scratch_shapes=[pltpu.VMEM((NSLAB, POOL_HALO + TM_MIX, LANES), F32),
                        pltpu.VMEM((NSLAB, CONV_HALO + TM_MIX, LANES), F32),
                        pltpu.VMEM((NSLAB, TM_MIX, LANES), F32),
                        pltpu.VMEM((TM_MIX, D), BF),
                        pltpu.VMEM((len(WINDOWS), TM_MIX, GD), BF),
                        pltpu.VMEM((N_GATES, len(WINDOWS), TM_MIX, GD), F32)],
        input_output_aliases={0: 0},
        compiler_params=pltpu.CompilerParams(
            dimension_semantics=("arbitrary", "arbitrary"), vmem_limit_bytes=VMEM_LIMIT),
        name="prompt_mix",
    )(x, mix_norm, w_in, w_pool, pool_scale, w_dw, b_dw, ln_g, ln_b, w_pw, w_out,
      memk_b, memv_b)


def _sample_pre_kernel(x_ref, g_ref, win_ref, wpool_ref, ps_ref, wdw_ref, bdw_ref, lng_ref,
                       lnb_ref, wpw_ref, sp_ref, sc_ref,
                       part_ref, q_ref, gc_ref, pool_o_ref, conv_o_ref, y_ref):
    sb, t = SB_PRE, DEC_SEQ
    slabs = GD // LANES
    x = x_ref[...].reshape(t * sb, D)
    h = _rms(x, g_ref[...]).astype(BF)
    pos = PAST_LEN + lax.broadcasted_iota(jnp.int32, (t, 1, 1), 0)

    def proj(kind, gi):
        col = kind * D + gi * GD
        return _dot(h, win_ref[:, col:col + GD])

    pool_o_ref[0:PB - t] = sp_ref[t:PB]
    conv_o_ref[0:CB - t] = sc_ref[t:CB]

    pool_part, gate_b = [], []
    for gi, w in enumerate(WINDOWS):
        cols = slice(gi * GD, (gi + 1) * GD)
        z = proj(1, gi) * jax.nn.sigmoid(proj(2, gi))
        conv_o_ref[CB - t:CB, :, cols] = z.reshape(t, sb, GD)
        for cb in range(gi * slabs, (gi + 1) * slabs):
            for s0 in range(0, sb, SUBLANES):
                rows = slice(s0, s0 + SUBLANES)
                acc = None
                for k in range(CK):
                    lo = sc_ref[k:t, rows, _slab(cb)] if k < t else None
                    hi = conv_o_ref[max(k, t) - t:k, rows, _slab(cb)] if k > 0 else None
                    win = lo if hi is None else hi if lo is None else jnp.concatenate([lo, hi], 0)
                    term = win * wdw_ref[k:k + 1, _slab(cb)]
                    acc = term if acc is None else acc + term
                y_ref[:, rows, _slab(cb)] = acc
        u3 = proj(0, gi).reshape(t, sb, GD)
        pool_o_ref[PB - t:PB, :, cols] = u3
        s = u3
        for j in range(1, w):
            prev = sp_ref[PB - j:PB - j + min(j, t), :, cols]
            s = s + (prev if j >= t else jnp.concatenate([prev, u3[0:t - j]], axis=0))
        inv_cnt = 1.0 / jnp.minimum(pos + 1, w).astype(F32)
        pooled = (s * inv_cnt - u3).reshape(t * sb, GD).astype(BF)
        y_pool = _dot(pooled, wpool_ref[gi]) * ps_ref[:, cols]
        pool_part.append(jax.nn.sigmoid(proj(4, gi)) * y_pool)
        gate_b.append(jax.nn.sigmoid(proj(5, gi)))
        q_ref[:, :, cols] = proj(3, gi).reshape(t, sb, GD)
        gc_ref[:, :, cols] = jax.nn.sigmoid(proj(6, gi)).reshape(t, sb, GD)

    y_conv = _conv_ln_pw(y_ref[...].reshape(t * sb, D), bdw_ref, lng_ref, lnb_ref, wpw_ref)
    for gi in range(len(WINDOWS)):
        cols = slice(gi * GD, (gi + 1) * GD)
        part_ref[:, :, cols] = (pool_part[gi] + gate_b[gi] * y_conv[:, cols]).reshape(t, sb, GD)


def _sample_pre(xs, layer, mix_norm, w_in, w_pool, pool_scale, w_dw, b_dw, ln_g, ln_b, w_pw,
                state_pool_t, state_conv_t):
    vec = _const_spec((1, D), layer)
    tok = pl.BlockSpec((DEC_SEQ, SB_PRE, D), lambda i: (0, i, 0))
    tok_in = pl.BlockSpec((DEC_SEQ, SB_PRE, D), lambda i: (SAMPLE_PLANE_BLOCK, i, 0))
    tok_sds = jax.ShapeDtypeStruct((DEC_SEQ, DEC_BATCH, D), F32)
    return pl.pallas_call(
        _sample_pre_kernel,
        grid=(DEC_BATCH // SB_PRE,),
        in_specs=[tok_in, vec, _const_spec((D, 7 * D), layer),
                  _const_spec((len(WINDOWS), GD, GD), layer),
                  vec, _const_spec((CK, D), layer), vec, vec, vec, _const_spec((D, D), layer),
                  pl.BlockSpec((None, PB, SB_PRE, D), lambda i: (layer, 0, i, 0)),
                  pl.BlockSpec((None, CB, SB_PRE, D), lambda i: (layer, 0, i, 0))],
        out_specs=[tok, tok, tok,
                   pl.BlockSpec((PB, SB_PRE, D), lambda i: (0, i, 0)),
                   pl.BlockSpec((CB, SB_PRE, D), lambda i: (0, i, 0))],
        out_shape=[tok_sds, tok_sds, tok_sds,
                   jax.ShapeDtypeStruct((PB, DEC_BATCH, D), F32),
                   jax.ShapeDtypeStruct((CB, DEC_BATCH, D), F32)],
        scratch_shapes=[pltpu.VMEM((DEC_SEQ, SB_PRE, D), F32)],
        compiler_params=pltpu.CompilerParams(
            dimension_semantics=("arbitrary",), vmem_limit_bytes=VMEM_LIMIT),
        name="sample_pre",
    )(xs, mix_norm, w_in, w_pool, pool_scale, w_dw, b_dw, ln_g, ln_b, w_pw,
      state_pool_t, state_conv_t)


def _sample_att_kernel(x_ref, part_ref, q_ref, gc_ref, k_ref, v_ref, wout_ref, xo_ref,
                       qs_ref, ys_ref):
    sb, t = SB_ATT, DEC_SEQ
    half = NH * t
    for cb in range(NSLAB):
        qs_ref[cb] = q_ref[:, :, _slab(cb)].reshape(t * sb, LANES)

    row_head = lax.broadcasted_iota(jnp.int32, (half, KV_ROWS), 0) // t
    lane_slot = lax.broadcasted_iota(jnp.int32, (half, KV_ROWS), 1) % SUBLANES
    own = lane_slot == row_head

    prods = []
    for s in range(sb):
        qst = jnp.concatenate(
            [qs_ref[pl.ds(2 * hd + j, 1, stride=2), pl.ds(s, t, stride=sb), :].reshape(t, LANES)
             for j in range(2) for hd in range(NH)], axis=0).astype(BF)
        prods.append(_dot_t(qst, k_ref[s].astype(BF)))
    for s in range(sb):
        prod = prods[s]
        sc = prod[0:half] + pltpu.roll(prod[half:2 * half], KV_ROWS - NH, axis=1)
        sc = jnp.where(own, sc * (DH ** -0.5), -jnp.inf)
        p = _softmax(sc)
        pst = jnp.concatenate([p, pltpu.roll(p, NH, axis=1)], axis=0).astype(BF)
        o = _dot(pst, v_ref[s].astype(BF))
        for j in range(2):
            for hd in range(NH):
                r0 = (j * NH + hd) * t
                ys_ref[pl.ds(2 * hd + j, 1, stride=2), pl.ds(s, t, stride=sb), :] = (
                    o[r0:r0 + t].reshape(1, t, LANES))

    y_att = jnp.concatenate([ys_ref[cb] for cb in range(NSLAB)], axis=-1)
    merged = part_ref[...].reshape(t * sb, D) + gc_ref[...].reshape(t * sb, D) * y_att
    out = x_ref[...].reshape(t * sb, D) + _dot(merged.astype(BF), wout_ref[...])
    xo_ref[...] = out.reshape(t, sb, D)


def _sample_att(xs, layer, part, q, gc, cache_k, cache_v, w_out):
    tok = pl.BlockSpec((DEC_SEQ, SB_ATT, D), lambda i: (0, i, 0))
    x_spec = pl.BlockSpec((DEC_SEQ, SB_ATT, D), lambda i: (SAMPLE_PLANE_BLOCK, i, 0))
    kv = pl.BlockSpec((None, SB_ATT, KV_ROWS, LANES), lambda i: (layer, i, 0, 0))
    return pl.pallas_call(
        _sample_att_kernel,
        grid=(DEC_BATCH // SB_ATT,),
        in_specs=[x_spec, tok, tok, tok, kv, kv, _const_spec((D, D), layer)],
        out_specs=x_spec,
        out_shape=jax.ShapeDtypeStruct(xs.shape, F32),
        scratch_shapes=[pltpu.VMEM((NSLAB, DEC_SEQ * SB_ATT, LANES), F32),
                        pltpu.VMEM((NSLAB, DEC_SEQ * SB_ATT, LANES), F32)],
        input_output_aliases={0: 0},
        compiler_params=pltpu.CompilerParams(
            dimension_semantics=("arbitrary",), vmem_limit_bytes=VMEM_LIMIT),
        name="sample_att",
    )(xs, part, q, gc, cache_k, cache_v, w_out)


def _cache_rows(c):
    c = c.reshape(DEPTH, DEC_BATCH, NMEM, NH, DH // LANES, LANES)
    return c.transpose(0, 1, 2, 4, 3, 5).reshape(DEPTH, DEC_BATCH, KV_ROWS, LANES)


def kernel(x_prompt, x_sample, mem_prompt, state_pool, state_conv, cache_mem_k, cache_mem_v,
           ffn1_norm, ffn1_w_gu, ffn1_w_down, mix_norm, w_in, mem_norm, w_mem_kv, w_pool,
           pool_scale, w_dw, b_dw, conv_ln_g, conv_ln_b, w_pw, w_out, ffn2_norm, ffn2_w_gu,
           ffn2_w_down, final_norm):
    assert x_prompt.shape == (BATCH, SEQ, D) and x_sample.shape == (DEC_BATCH, DEC_SEQ, D)
    x = [x_prompt.reshape(NP, D), x_sample.transpose(1, 0, 2).reshape(NS, D)]

    def vec(a):
        return a.reshape(DEPTH, 1, D)

    ffn1_norm, mix_norm, mem_norm, pool_scale, b_dw, conv_ln_g, conv_ln_b, ffn2_norm = map(
        vec, (ffn1_norm, mix_norm, mem_norm, pool_scale, b_dw, conv_ln_g, conv_ln_b, ffn2_norm))
    w_in, w_mem_kv, w_pool, w_pw, w_out = (
        a.astype(BF) for a in (w_in, w_mem_kv, w_pool, w_pw, w_out))
    cache_k = _cache_rows(cache_mem_k)
    cache_v = _cache_rows(cache_mem_v)
    state_pool_t = state_pool.transpose(0, 2, 1, 3)
    state_conv_t = state_conv.transpose(0, 2, 1, 3)

    memk, memv, memk_b, memv_b = _memory_kv(mem_prompt.reshape(BATCH * NMEM, D), mem_norm,
                                            w_mem_kv)

    pool_p, conv_p, pool_s, conv_s = [], [], [], []
    for l in range(DEPTH):
        x = _ffn(x, ffn1_norm, ffn1_w_gu, ffn1_w_down, l)
        mix_w = (mix_norm, w_in, w_pool, pool_scale, w_dw, b_dw, conv_ln_g, conv_ln_b, w_pw)
        planes = (NT // DEC_BATCH, DEC_BATCH, D)
        part, q, gc, spool, sconv = _sample_pre(x.reshape(planes), l, *mix_w,
                                                state_pool_t, state_conv_t)
        x, ppool, pconv = _prompt_mix(x, l, *mix_w, w_out, memk_b, memv_b)
        x = _sample_att(x.reshape(planes), l, part, q, gc, cache_k, cache_v, w_out).reshape(NT, D)
        pool_p.append(ppool)
        conv_p.append(pconv)
        pool_s.append(spool)
        conv_s.append(sconv)
        x = _ffn([x], ffn2_norm, ffn2_w_gu, ffn2_w_down, l,
                 final_norm=final_norm.reshape(1, D) if l == DEPTH - 1 else None)
        if l < DEPTH - 1:
            x = [x]

    y_prompt, y_sample = x
    y_sample = y_sample.reshape(DEC_SEQ, DEC_BATCH, D).transpose(1, 0, 2)
    return (y_prompt.reshape(BATCH, SEQ, D), y_sample,
            jnp.stack(pool_p), jnp.stack(conv_p), _cache_rows_inverse(memk),
            _cache_rows_inverse(memv),
            jnp.stack(pool_s).transpose(0, 2, 1, 3), jnp.stack(conv_s).transpose(0, 2, 1, 3))
```
